```python
import jax
import jax.numpy as jnp
from jax import lax
import numpy as np

D_MODEL = 2048
BATCH = 8
SEQ = 2048
DEPTH = 1
DEC_BATCH = 128
DEC_SEQ = 4
PAST_LEN = 2048
PAGE_SIZE = 128

D_MIX = D_MODEL
CONV_CH = D_MIX // 2
CONV_W = 31
N_HEADS = 8
HEAD_DIM = (D_MIX - CONV_CH) // N_HEADS
ATT_W = N_HEADS * HEAD_DIM
ROT_DIM = HEAD_DIM // 4
ROPE_THETA = 500000.0
WINDOWS = (128, 512, 2048)
DILATIONS = (1, 4, 16)
Q_BLOCK = 128
WIN_MAX = max(WINDOWS)
ATT_SCALE = HEAD_DIM ** -0.5
IN_COLS = 2 * CONV_CH + 3 * ATT_W
N_ADA = 6
PEER_HEADS = 8
PEER_NKEYS = 128
PEER_DK = 128
PEER_TOPK = 16
N_EXPERTS = PEER_NKEYS * PEER_NKEYS
TOK_BLOCK = 128
EPS = 1e-6

kernel_name = 'hymba_conformer_dilated_peer_decode_step'


def _rmsnorm(x, g):
    xf = x.astype(jnp.float32)
    y = xf * lax.rsqrt(jnp.mean(xf * xf, axis=-1, keepdims=True) + EPS)
    return (y * g.astype(jnp.float32)).astype(x.dtype)


def _rope(x, pos):
    inv = ROPE_THETA ** (-jnp.arange(0, ROT_DIM, 2, dtype=jnp.float32) / ROT_DIM)
    ang = pos.astype(jnp.float32)[:, None] * inv[None, :]
    cos = jnp.cos(ang)[None, :, None, :]
    sin = jnp.sin(ang)[None, :, None, :]
    xr = x[..., :ROT_DIM].astype(jnp.float32)
    x1, x2 = xr[..., :ROT_DIM // 2], xr[..., ROT_DIM // 2:]
    rot = jnp.concatenate([x1 * cos - x2 * sin, x2 * cos + x1 * sin], axis=-1).astype(x.dtype)
    return jnp.concatenate([rot, x[..., ROT_DIM:]], axis=-1)


def _dwconv(xp, w, b):
    y = lax.conv_general_dilated(xp, w[:, None, :], (1,), 'VALID',
                                 dimension_numbers=('NWC', 'WIO', 'NWC'),
                                 feature_group_count=xp.shape[-1])
    return y + b


def _conv_tail(y, g, b):
    yf = y.astype(jnp.float32)
    mu = jnp.mean(yf, axis=-1, keepdims=True)
    var = jnp.mean(jnp.square(yf - mu), axis=-1, keepdims=True)
    z = (yf - mu) * lax.rsqrt(var + EPS) * g.astype(jnp.float32) + b.astype(jnp.float32)
    return jax.nn.silu(z).astype(y.dtype)


def _dilated_prompt(q, k, v, win, dil):
    B, S, H, E = q.shape
    nk = win // dil
    M = S // dil
    n_blk = -(-M // Q_BLOCK)
    Mp = n_blk * Q_BLOCK

    def by_res(t):
        return t.reshape(B, M, dil, H, E).transpose(0, 2, 1, 3, 4)

    qb = jnp.pad(by_res(q), ((0, 0), (0, 0), (0, Mp - M), (0, 0), (0, 0))).reshape(B, dil, n_blk, Q_BLOCK, H, E)

    def band(t):
        tp = jnp.pad(by_res(t), ((0, 0), (0, 0), (Q_BLOCK, Mp - M), (0, 0), (0, 0)))
        prev = tp[:, :, :Mp].reshape(B, dil, n_blk, Q_BLOCK, H, E)
        cur = tp[:, :, Q_BLOCK:].reshape(B, dil, n_blk, Q_BLOCK, H, E)
        return jnp.concatenate([prev, cur], axis=3)

    kb, vb = band(k), band(v)
    s = jnp.einsum('brnqhe,brnkhe->brnhqk', qb, kb).astype(jnp.float32) * ATT_SCALE
    qi = jnp.arange(Q_BLOCK)[:, None]
    kj = jnp.arange(2 * Q_BLOCK)[None, :]
    dist = qi + Q_BLOCK - kj
    key_m = jnp.arange(n_blk)[:, None, None] * Q_BLOCK - Q_BLOCK + kj[None]
    valid = ((dist >= 0) & (dist <= nk))[None] & (key_m >= 0)
    s = jnp.where(valid[None, None, :, None], s, -jnp.inf)
    lse = jax.nn.logsumexp(s, axis=-1)
    p = jnp.exp(s - lse[..., None]).astype(v.dtype)
    o = jnp.einsum('brnhqk,brnkhe->brnqhe', p, vb)
    o = o.reshape(B, dil, Mp, H, E)[:, :, :M].transpose(0, 2, 1, 3, 4).reshape(B, S, H, E)
    lse = lse.transpose(0, 1, 2, 4, 3).reshape(B, dil, Mp, H)[:, :, :M].transpose(0, 2, 1, 3).reshape(B, S, H)
    return o, lse


def _dilated_sample(q, k_new, v_new, k_buf, v_buf, win, dil):
    B, Sn, H, E = q.shape
    L = k_buf.shape[1]
    nk = win // dil
    idx = L + jnp.arange(Sn)[:, None] - dil * jnp.arange(nk + 1)[None, :]
    sel = (idx >= L)[None, :, :, None, None]
    ic = jnp.clip(idx, 0, L - 1)
    inn = jnp.clip(idx - L, 0, Sn - 1)
    kg = jnp.where(sel, k_new[:, inn], k_buf[:, ic])
    vg = jnp.where(sel, v_new[:, inn], v_buf[:, ic])
    s = jnp.einsum('bshe,bskhe->bhsk', q, kg).astype(jnp.float32) * ATT_SCALE
    s = jnp.where((idx >= 0)[None, None], s, -jnp.inf)
    lse = jax.nn.logsumexp(s, axis=-1)
    p = jnp.exp(s - lse[..., None]).astype(v_new.dtype)
    o = jnp.einsum('bhsk,bskhe->bshe', p, vg)
    return o, lse.transpose(0, 2, 1)


def _merge(outs):
    o = jnp.stack([oo for oo, _ in outs])
    lse = jnp.stack([ll for _, ll in outs])
    w = jax.nn.softmax(lse, axis=0)
    return jnp.einsum('pbsh,pbshe->bshe', w.astype(o.dtype), o)


def _peer(h, wq, keys, u_tab, v_tab):
    shp = h.shape
    t = h.reshape(-1, shp[-1])
    T = t.shape[0]
    n = -(-T // TOK_BLOCK)
    t = jnp.pad(t, ((0, n * TOK_BLOCK - T), (0, 0)))

    def blk(tb):
        q = (tb @ wq).reshape(TOK_BLOCK, PEER_HEADS, 2, PEER_DK)
        s = jnp.einsum('thpd,hpkd->thpk', q, keys).astype(jnp.float32)
        sv, si = lax.top_k(s, PEER_TOPK)
        cand = (sv[:, :, 0, :, None] + sv[:, :, 1, None, :]).reshape(TOK_BLOCK, PEER_HEADS, PEER_TOPK * PEER_TOPK)
        fv, fi = lax.top_k(cand, PEER_TOPK)
        i1 = jnp.take_along_axis(si[:, :, 0], fi // PEER_TOPK, axis=-1)
        i2 = jnp.take_along_axis(si[:, :, 1], fi % PEER_TOPK, axis=-1)
        eid = (i1 * PEER_NKEYS + i2).reshape(TOK_BLOCK, PEER_HEADS * PEER_TOPK)
        g = jax.nn.softmax(fv, axis=-1).reshape(TOK_BLOCK, PEER_HEADS * PEER_TOPK)
        hid = jax.nn.gelu(jnp.einsum('td,ted->te', tb, u_tab[eid]).astype(jnp.float32), approximate=False)
        return jnp.einsum('te,ted->td', (g * hid).astype(tb.dtype), v_tab[eid])

    out = lax.map(blk, t.reshape(n, TOK_BLOCK, shp[-1])).reshape(n * TOK_BLOCK, shp[-1])[:T]
    return out.reshape(shp)


def _mixer_inputs(x, c, pos, w_ada, b_ada, norm1_g, w_in):
    ada = (jax.nn.silu(c) @ w_ada + b_ada)[:, None, :]
    mods = jnp.split(ada, N_ADA, axis=-1)
    h = _rmsnorm(x, norm1_g) * (1.0 + mods[1]) + mods[0]
    z = h @ w_in
    a, gt, q, k, v = jnp.split(z, [CONV_CH, 2 * CONV_CH, 2 * CONV_CH + ATT_W, 2 * CONV_CH + 2 * ATT_W], axis=-1)
    glu = a * jax.nn.sigmoid(gt)

    def heads(t):
        return t.reshape(*t.shape[:-1], N_HEADS, HEAD_DIM)

    return mods, glu, _rope(heads(q), pos), _rope(heads(k), pos), heads(v)


def _finish(x, mods, conv_raw, attn, conv_ln_g, conv_ln_b, attn_out_g, w_out,
            norm2_g, peer_wq, peer_keys, peer_u, peer_v):
    conv_y = _conv_tail(conv_raw, conv_ln_g, conv_ln_b)
    a = _rmsnorm(attn, attn_out_g)
    a = a.reshape(*a.shape[:-2], ATT_W)
    mix = jnp.concatenate([conv_y, a], axis=-1) @ w_out
    x = x + (1.0 + mods[2]) * mix
    h = _rmsnorm(x, norm2_g) * (1.0 + mods[4]) + mods[3]
    return x + (1.0 + mods[5]) * _peer(h, peer_wq, peer_keys, peer_u, peer_v)


def setup_inputs(seed: int = 0) -> dict:
    key = jax.random.key(seed)
    ks = jax.random.split(key, 24)

    def nrm(k, shape, scale):
        return jax.random.normal(k, shape, jnp.float32) * scale

    lw = min(WIN_MAX, PAST_LEN)
    return {
        'x_prompt': nrm(ks[0], (BATCH, SEQ, D_MODEL), 1.0),
        'x_sample': nrm(ks[1], (DEC_BATCH, DEC_SEQ, D_MODEL), 1.0),
        'cache_k': nrm(ks[2], (DEPTH, DEC_BATCH, lw, N_HEADS, HEAD_DIM), 1.0),
        'cache_v': nrm(ks[3], (DEPTH, DEC_BATCH, lw, N_HEADS, HEAD_DIM), 1.0),
        'state_conv': nrm(ks[4], (DEPTH, DEC_BATCH, CONV_W - 1, CONV_CH), 0.5),
        'c_prompt': nrm(ks[5], (BATCH, D_MODEL), 1.0),
        'c_sample': nrm(ks[6], (DEC_BATCH, D_MODEL), 1.0),
        'w_ada': nrm(ks[7], (DEPTH, D_MODEL, N_ADA * D_MODEL), 0.1 * D_MODEL ** -0.5),
        'b_ada': nrm(ks[8], (DEPTH, N_ADA * D_MODEL), 0.01),
        'norm1_g': 1.0 + nrm(ks[9], (DEPTH, D_MODEL), 0.02),
        'w_in': nrm(ks[10], (DEPTH, D_MODEL, IN_COLS), D_MODEL ** -0.5),
        'conv_w': nrm(ks[11], (DEPTH, CONV_W, CONV_CH), CONV_W ** -0.5),
        'conv_b': nrm(ks[12], (DEPTH, CONV_CH), 0.01),
        'conv_ln_g': 1.0 + nrm(ks[13], (DEPTH, CONV_CH), 0.02),
        'conv_ln_b': nrm(ks[14], (DEPTH, CONV_CH), 0.01),
        'attn_out_g': 1.0 + nrm(ks[15], (DEPTH, N_HEADS, HEAD_DIM), 0.02),
        'w_out': nrm(ks[16], (DEPTH, D_MIX, D_MODEL), D_MIX ** -0.5),
        'norm2_g': 1.0 + nrm(ks[17], (DEPTH, D_MODEL), 0.02),
        'peer_wq': nrm(ks[18], (DEPTH, D_MODEL, PEER_HEADS * 2 * PEER_DK), D_MODEL ** -0.5),
        'peer_keys': nrm(ks[19], (DEPTH, PEER_HEADS, 2, PEER_NKEYS, PEER_DK), PEER_DK ** -0.5),
        'peer_u': nrm(ks[20], (DEPTH, N_EXPERTS, D_MODEL), D_MODEL ** -0.5),
        'peer_v': nrm(ks[21], (DEPTH, N_EXPERTS, D_MODEL), PEER_HEADS ** -0.5),
        'final_g': 1.0 + nrm(ks[22], (D_MODEL,), 0.02),
    }


def reference(x_prompt, x_sample, cache_k, cache_v, state_conv, c_prompt, c_sample,
              w_ada, b_ada, norm1_g, w_in, conv_w, conv_b, conv_ln_g, conv_ln_b, attn_out_g, w_out,
              norm2_g, peer_wq, peer_keys, peer_u, peer_v, final_g):
    xp, xs = x_prompt, x_sample
    S = x_prompt.shape[1]
    pos_p = jnp.arange(S)
    pos_s = PAST_LEN + jnp.arange(x_sample.shape[1])
    keep = min(WIN_MAX, S)
    pk, pv, pc, sk, sv, sc = [], [], [], [], [], []
    for l in range(DEPTH):
        mods, glu, q, k, v = _mixer_inputs(xp, c_prompt, pos_p, w_ada[l], b_ada[l], norm1_g[l], w_in[l])
        conv_raw = _dwconv(jnp.pad(glu, ((0, 0), (CONV_W - 1, 0), (0, 0))), conv_w[l], conv_b[l])
        attn = _merge([_dilated_prompt(q, k, v, w, d) for w, d in zip(WINDOWS, DILATIONS)])
        xp = _finish(xp, mods, conv_raw, attn, conv_ln_g[l], conv_ln_b[l], attn_out_g[l], w_out[l],
                     norm2_g[l], peer_wq[l], peer_keys[l], peer_u[l], peer_v[l])
        pk.append(k[:, S - keep:])
        pv.append(v[:, S - keep:])
        pc.append(glu[:, S - (CONV_W - 1):])
        mods, glu, q, k, v = _mixer_inputs(xs, c_sample, pos_s, w_ada[l], b_ada[l], norm1_g[l], w_in[l])
        hist = jnp.concatenate([state_conv[l].astype(glu.dtype), glu], axis=1)
        conv_raw = _dwconv(hist, conv_w[l], conv_b[l])
        attn = _merge([_dilated_sample(q, k, v, cache_k[l], cache_v[l], w, d) for w, d in zip(WINDOWS, DILATIONS)])
        xs = _finish(xs, mods, conv_raw, attn, conv_ln_g[l], conv_ln_b[l], attn_out_g[l], w_out[l],
                     norm2_g[l], peer_wq[l], peer_keys[l], peer_u[l], peer_v[l])
        sk.append(k)
        sv.append(v)
        sc.append(hist[:, hist.shape[1] - (CONV_W - 1):])
    y_prompt = _rmsnorm(xp, final_g)
    y_sample = _rmsnorm(xs, final_g)
    return (y_prompt, y_sample, jnp.stack(pk), jnp.stack(pv), jnp.stack(pc),
            jnp.stack(sk), jnp.stack(sv), jnp.stack(sc))
```

```python
import functools

import numpy as np
import jax
import jax.numpy as jnp
from jax import lax
from jax.experimental import pallas as pl
from jax.experimental.pallas import tpu as pltpu

F32 = jnp.float32
BF16 = jnp.bfloat16

D_MODEL = 2048
CONV_CH = 1024
CONV_W = 31
N_HEADS = 8
HEAD_DIM = 128
ATT_W = N_HEADS * HEAD_DIM
ROT_DIM = HEAD_DIM // 4
ROPE_THETA = 500000.0
WINDOWS = (128, 512, 2048)
DILATIONS = (1, 4, 16)
Q_BLOCK = 128
ATT_SCALE = HEAD_DIM ** -0.5
N_ADA = 6
PEER_HEADS = 8
PEER_NKEYS = 128
PEER_DK = 128
PEER_TOPK = 16
EPS = 1e-6

LANES = 128
SUBLANES = 8
MIB = 1024 * 1024

NEG_INF = float("-inf")


def _nt_dot(a, b):
    return lax.dot_general(a, b, (((1,), (1,)), ((), ())), preferred_element_type=F32)


def _rows(ref):
    return ref[0] if len(ref.shape) == 3 else ref[...]


def _params(sem, vmem_mib):
    return pltpu.CompilerParams(dimension_semantics=sem, vmem_limit_bytes=vmem_mib * MIB)


def _ada_kernel(c_ref, w_ref, b_ref, o_ref):
    c = c_ref[...]
    a = (c * jax.nn.sigmoid(c)).astype(BF16)
    o_ref[...] = jnp.dot(a, w_ref[...].astype(BF16), preferred_element_type=F32) + b_ref[...]


def _ada(c_rows, w_ada, b_ada):
    rows = c_rows.shape[0]
    n = w_ada.shape[1]
    tn = 1024
    return pl.pallas_call(
        _ada_kernel,
        grid=(n // tn,),
        in_specs=[pl.BlockSpec((rows, D_MODEL), lambda j: (0, 0)),
                  pl.BlockSpec((D_MODEL, tn), lambda j: (0, j)),
                  pl.BlockSpec((1, tn), lambda j: (0, j))],
        out_specs=pl.BlockSpec((rows, tn), lambda j: (0, j)),
        out_shape=jax.ShapeDtypeStruct((rows, n), F32),
        compiler_params=_params(("arbitrary",), 48),
        name="ada",
    )(c_rows, w_ada, b_ada.reshape(1, n))


def _rope_heads(z, cos, sin_lo, sin_hi):
    outs = []
    for h in range(N_HEADS):
        zh = z[:, h * HEAD_DIM:(h + 1) * HEAD_DIM]
        up = pltpu.roll(zh, HEAD_DIM - ROT_DIM // 2, 1)
        dn = pltpu.roll(zh, ROT_DIM // 2, 1)
        outs.append(zh * cos + up * sin_lo + dn * sin_hi)
    return jnp.concatenate(outs, axis=1)


def _mixin_kernel(x_ref, sh_ref, sc_ref, g_ref, w_ref, cos_ref, slo_ref, shi_ref,
                  glu_ref, q_ref, k_ref, v_ref, h_scr, a_scr):
    j = pl.program_id(1)

    @pl.when(j == 0)
    def _():
        x = x_ref[...]
        y = x * lax.rsqrt(jnp.mean(x * x, axis=-1, keepdims=True) + EPS)
        h = (y * g_ref[...]) * (1.0 + _rows(sc_ref)) + _rows(sh_ref)
        h_scr[...] = h.astype(BF16)

    z = jnp.dot(h_scr[...], w_ref[...], preferred_element_type=F32)

    @pl.when(j == 0)
    def _():
        a_scr[...] = z

    @pl.when(j == 1)
    def _():
        glu_ref[...] = a_scr[...] * jax.nn.sigmoid(z)

    @pl.when(j == 2)
    def _():
        q_ref[...] = _rope_heads(z, cos_ref[...], slo_ref[...], shi_ref[...])

    @pl.when(j == 3)
    def _():
        k_ref[...] = _rope_heads(z, cos_ref[...], slo_ref[...], shi_ref[...])

    @pl.when(j == 4)
    def _():
        v_ref[...] = z


def _rope_tables(pos):
    inv = ROPE_THETA ** (-np.arange(0, ROT_DIM, 2, dtype=np.float32) / ROT_DIM)
    ang = (pos.astype(np.float32)[:, None] * inv[None, :]).astype(np.float32)
    cos, sin = np.cos(ang), np.sin(ang)
    half = ROT_DIM // 2
    n = pos.shape[0]
    c = np.ones((n, HEAD_DIM), np.float32)
    c[:, :half] = cos
    c[:, half:ROT_DIM] = cos
    s_lo = np.zeros((n, HEAD_DIM), np.float32)
    s_lo[:, :half] = -sin
    s_hi = np.zeros((n, HEAD_DIM), np.float32)
    s_hi[:, half:ROT_DIM] = sin
    return jnp.asarray(c), jnp.asarray(s_lo), jnp.asarray(s_hi)


def _mod_spec(mods, col, tm, group_rows, ngrid):
    if mods.ndim == 3:
        assert group_rows % tm == 0
        per_group = group_rows // tm
        if ngrid == 1:
            return pl.BlockSpec((1, 1, D_MODEL), lambda i: (i // per_group, 0, col))
        return pl.BlockSpec((1, 1, D_MODEL), lambda i, j: (i // per_group, 0, col))
    if ngrid == 1:
        return pl.BlockSpec((tm, D_MODEL), lambda i: (i, col))
    return pl.BlockSpec((tm, D_MODEL), lambda i, j: (i, col))


def _mixin(x, mods, group_rows, norm_g, w_in_bf, tables, tm):
    t = x.shape[0]
    cos, s_lo, s_hi = tables
    pos_blocks = cos.shape[0] // tm
    tab_spec = pl.BlockSpec((tm, HEAD_DIM), lambda i, j: (i % pos_blocks, 0))
    out_spec = pl.BlockSpec((tm, ATT_W), lambda i, j: (i, 0))
    out_sds = jax.ShapeDtypeStruct((t, ATT_W), F32)
    return pl.pallas_call(
        _mixin_kernel,
        grid=(t // tm, 5),
        in_specs=[pl.BlockSpec((tm, D_MODEL), lambda i, j: (i, 0)),
                  _mod_spec(mods, 0, tm, group_rows, 2),
                  _mod_spec(mods, 1, tm, group_rows, 2),
                  pl.BlockSpec((1, D_MODEL), lambda i, j: (0, 0)),
                  pl.BlockSpec((D_MODEL, ATT_W), lambda i, j: (0, j)),
                  tab_spec, tab_spec, tab_spec],
        out_specs=[out_spec, out_spec, out_spec, out_spec],
        out_shape=[out_sds, out_sds, out_sds, out_sds],
        scratch_shapes=[pltpu.VMEM((tm, D_MODEL), BF16), pltpu.VMEM((tm, ATT_W), F32)],
        compiler_params=_params(("arbitrary", "arbitrary"), 56),
        name="mixin",
    )(x, mods, mods, norm_g.reshape(1, D_MODEL), w_in_bf, cos, s_lo, s_hi)


CONV_HIST = CONV_W - 1
CONV_PAD = 32
CONV_ROWS = 64


def _conv_prompt_kernel(g_ref, hist_ref, w_ref, b_ref, o_ref, xp_scr):
    seq = g_ref.shape[1]
    xp_scr[0:CONV_PAD, :] = hist_ref[0]
    xp_scr[CONV_PAD:CONV_PAD + seq, :] = g_ref[0]
    lead = CONV_PAD - CONV_HIST

    def body(c, carry):
        t0 = pl.multiple_of(c * CONV_ROWS, CONV_ROWS)
        win = xp_scr[pl.ds(t0, CONV_ROWS + CONV_PAD), :]
        acc = jnp.zeros((CONV_ROWS, LANES), F32) + b_ref[...]
        for sub in range(SUBLANES):
            shifted = win if sub == 0 else pltpu.roll(win, CONV_ROWS + CONV_PAD - sub, 0)
            for a in range(CONV_PAD // SUBLANES + 1):
                off = a * SUBLANES + sub
                j = off - lead
                if 0 <= j < CONV_W:
                    acc = acc + shifted[a * SUBLANES:a * SUBLANES + CONV_ROWS, :] * w_ref[j:j + 1, :]
        o_ref[0, pl.ds(t0, CONV_ROWS), :] = acc
        return carry

    lax.fori_loop(0, seq // CONV_ROWS, body, 0)


def _conv_prompt(glu3, hist_pad, conv_w, conv_b):
    b, seq, ch = glu3.shape
    return pl.pallas_call(
        _conv_prompt_kernel,
        grid=(b, ch // LANES),
        in_specs=[pl.BlockSpec((1, seq, LANES), lambda i, c: (i, 0, c)),
                  pl.BlockSpec((1, CONV_PAD, LANES), lambda i, c: (i, 0, c)),
                  pl.BlockSpec((CONV_W, LANES), lambda i, c: (0, c)),
                  pl.BlockSpec((1, LANES), lambda i, c: (0, c))],
        out_specs=pl.BlockSpec((1, seq, LANES), lambda i, c: (i, 0, c)),
        out_shape=jax.ShapeDtypeStruct((b, seq, ch), F32),
        scratch_shapes=[pltpu.VMEM((seq + CONV_PAD, LANES), F32)],
        compiler_params=_params(("arbitrary", "arbitrary"), 32),
        name="conv_prompt",
    )(glu3, hist_pad, conv_w, conv_b.reshape(1, ch))


def _conv_sample_kernel(hist_ref, g_ref, wh_ref, wn_ref, b_ref, o_ref):
    hist = hist_ref[...]
    g = g_ref[...]
    n_new = g.shape[1]
    for t in range(n_new):
        y = jnp.sum(hist * wh_ref[t][None], axis=1) + jnp.sum(g * wn_ref[t][None], axis=1)
        o_ref[t] = y + b_ref[...]


def _conv_sample(state, glu3, conv_w, conv_b):
    b, n_new, ch = glu3.shape
    wh = jnp.stack([jnp.pad(conv_w[:CONV_HIST - t], ((t, 0), (0, 0))) for t in range(n_new)])
    wn = jnp.stack([jnp.pad(conv_w[CONV_HIST - t:], ((0, n_new - 1 - t), (0, 0))) for t in range(n_new)])
    nb = 32
    return pl.pallas_call(
        _conv_sample_kernel,
        grid=(b // nb, ch // LANES),
        in_specs=[pl.BlockSpec((nb, CONV_HIST, LANES), lambda i, c: (i, 0, c)),
                  pl.BlockSpec((nb, n_new, LANES), lambda i, c: (i, 0, c)),
                  pl.BlockSpec((n_new, CONV_HIST, LANES), lambda i, c: (0, 0, c)),
                  pl.BlockSpec((n_new, n_new, LANES), lambda i, c: (0, 0, c)),
                  pl.BlockSpec((1, LANES), lambda i, c: (0, c))],
        out_specs=pl.BlockSpec((n_new, nb, LANES), lambda i, c: (0, i, c)),
        out_shape=jax.ShapeDtypeStruct((n_new, b, ch), F32),
        compiler_params=_params(("arbitrary", "arbitrary"), 32),
        name="conv_sample",
    )(state, glu3, wh, wn, conv_b.reshape(1, ch)).transpose(1, 0, 2)


def _head_norm(o, g):
    return o * lax.rsqrt(jnp.mean(o * o, axis=-1, keepdims=True) + EPS) * g


def _attn_prompt_kernel(q_ref, k_ref, v_ref, g_ref, o_ref, acc_scr, m_scr, l_scr):
    seq = q_ref.shape[1]
    qi = lax.broadcasted_iota(jnp.int32, (Q_BLOCK, Q_BLOCK), 0)
    kj = lax.broadcasted_iota(jnp.int32, (Q_BLOCK, Q_BLOCK), 1)
    cur_ok = kj <= qi

    for win, dil in zip(WINDOWS, DILATIONS):
        assert win // dil == Q_BLOCK
        n_blk = seq // dil // Q_BLOCK
        first = dil == DILATIONS[0]

        def rows(ref, start, dil=dil):
            if dil == 1:
                return ref[0, pl.ds(start, Q_BLOCK), :]
            return ref[0, pl.ds(start, Q_BLOCK, stride=dil), :]

        def put(ref, start, val, dil=dil):
            if dil == 1:
                ref[pl.ds(start, Q_BLOCK), :] = val
            else:
                ref[pl.ds(start, Q_BLOCK, stride=dil), :] = val

        def get(ref, start, dil=dil):
            if dil == 1:
                return ref[pl.ds(start, Q_BLOCK), :]
            return ref[pl.ds(start, Q_BLOCK, stride=dil), :]

        def body(i, carry, dil=dil, n_blk=n_blk, first=first, rows=rows, put=put, get=get):
            r = i // n_blk
            n = i % n_blk
            cur = r + dil * Q_BLOCK * n
            prev = r + dil * Q_BLOCK * jnp.maximum(n - 1, 0)
            if dil == 1:
                cur = pl.multiple_of(cur, Q_BLOCK)
                prev = pl.multiple_of(prev, Q_BLOCK)
            qb = rows(q_ref, cur).astype(BF16)
            kc = rows(k_ref, cur).astype(BF16)
            vc = rows(v_ref, cur).astype(BF16)
            kp = rows(k_ref, prev).astype(BF16)
            vp = rows(v_ref, prev).astype(BF16)
            s_c = jnp.where(cur_ok, _nt_dot(qb, kc) * ATT_SCALE, NEG_INF)
            no_prev = jnp.where(n > 0, 0, Q_BLOCK)
            s_p = jnp.where(kj >= qi + no_prev, _nt_dot(qb, kp) * ATT_SCALE, NEG_INF)
            m = jnp.maximum(jnp.max(s_c, axis=-1, keepdims=True), jnp.max(s_p, axis=-1, keepdims=True))
            p_c = jnp.exp(s_c - m)
            p_p = jnp.exp(s_p - m)
            l = jnp.sum(p_c, axis=-1, keepdims=True) + jnp.sum(p_p, axis=-1, keepdims=True)
            acc = (jnp.dot(p_c.astype(BF16), vc, preferred_element_type=F32)
                   + jnp.dot(p_p.astype(BF16), vp, preferred_element_type=F32))
            m_b = jnp.broadcast_to(m, (Q_BLOCK, HEAD_DIM))
            l_b = jnp.broadcast_to(l, (Q_BLOCK, HEAD_DIM))
            if not first:
                m_old = get(m_scr, cur)
                m_new = jnp.maximum(m_old, m_b)
                alpha = jnp.exp(m_old - m_new)
                beta = jnp.exp(m_b - m_new)
                acc = alpha * get(acc_scr, cur) + beta * acc
                l_b = alpha * get(l_scr, cur) + beta * l_b
                m_b = m_new
            put(acc_scr, cur, acc)
            put(m_scr, cur, m_b)
            put(l_scr, cur, l_b)
            return carry

        lax.fori_loop(0, dil * n_blk, body, 0)

    o_ref[0] = _head_norm(acc_scr[...] / l_scr[...], g_ref[0])


def _attn_prompt(q3, k3, v3, head_g):
    b, seq, _ = q3.shape
    spec = pl.BlockSpec((1, seq, HEAD_DIM), lambda i, h: (i, 0, h))
    return pl.pallas_call(
        _attn_prompt_kernel,
        grid=(b, N_HEADS),
        in_specs=[spec, spec, spec, pl.BlockSpec((1, 1, HEAD_DIM), lambda i, h: (h, 0, 0))],
        out_specs=spec,
        out_shape=jax.ShapeDtypeStruct((b, seq, ATT_W), F32),
        scratch_shapes=[pltpu.VMEM((seq, HEAD_DIM), F32)] * 3,
        compiler_params=_params(("arbitrary", "arbitrary"), 32),
        name="attn_prompt",
    )(q3, k3, v3, head_g.reshape(N_HEADS, 1, HEAD_DIM))


S_PAD = 8
TAIL = 512
N_COLS = LANES


def _attn_sample_kernel(q_ref, kn_ref, vn_ref, kt_ref, vt_ref, kd_ref, vd_ref, g_ref, o_ref, *, n_new, past):
    q8 = q_ref[0]
    qt = jnp.concatenate([q8] * N_HEADS + [jnp.zeros_like(q8)] * (N_COLS // S_PAD - N_HEADS), axis=0)
    row = lax.broadcasted_iota(jnp.int32, (N_COLS, ATT_W), 0)
    lane = lax.broadcasted_iota(jnp.int32, (N_COLS, ATT_W), 1)
    qbd = jnp.where(row // S_PAD == lane // HEAD_DIM, qt, 0.0).astype(BF16)

    def col_query(shape):
        return lax.broadcasted_iota(jnp.int32, shape, 1) % S_PAD % n_new

    def key_row(shape):
        return lax.broadcasted_iota(jnp.int32, shape, 0)

    groups = []
    kt = kt_ref[0].astype(BF16)
    s = _nt_dot(kt, qbd) * ATT_SCALE
    r, c = key_row(s.shape), col_query(s.shape)
    base = past - TAIL
    mult = ((((base + r) % DILATIONS[1] == (past + c) % DILATIONS[1])
             & (base + r >= past + c - WINDOWS[1])).astype(F32)
            + (base + r >= past + c - WINDOWS[0]).astype(F32))
    groups.append((s, mult, vt_ref[0].astype(BF16)))
    for res in range(n_new):
        kd = kd_ref[0, :, res * ATT_W:(res + 1) * ATT_W].astype(BF16)
        s = _nt_dot(kd, qbd) * ATT_SCALE
        mult = (col_query(s.shape) == res).astype(F32)
        groups.append((s, mult, vd_ref[0, :, res * ATT_W:(res + 1) * ATT_W].astype(BF16)))
    s = _nt_dot(kn_ref[0].astype(BF16), qbd) * ATT_SCALE
    r, c = key_row(s.shape), col_query(s.shape)
    mult = (r <= c).astype(F32) + 2.0 * (r == c).astype(F32)
    groups.append((s, mult, vn_ref[0].astype(BF16)))

    m = None
    for s, mult, _ in groups:
        gm = jnp.max(jnp.where(mult > 0, s, NEG_INF), axis=0, keepdims=True)
        m = gm if m is None else jnp.maximum(m, gm)
    num = jnp.zeros((N_COLS, ATT_W), F32)
    den = jnp.zeros((N_COLS, LANES), F32)
    for s, mult, v in groups:
        p = mult * jnp.exp(jnp.where(mult > 0, s - m, NEG_INF))
        pt = p.T.astype(BF16)
        num = num + jnp.dot(pt, v, preferred_element_type=F32)
        den = den + jnp.dot(pt, jnp.ones((v.shape[0], LANES), BF16), preferred_element_type=F32)
    outs = []
    for h in range(N_HEADS):
        o = (num[h * S_PAD:(h + 1) * S_PAD, h * HEAD_DIM:(h + 1) * HEAD_DIM]
             / den[h * S_PAD:(h + 1) * S_PAD, :])
        outs.append(_head_norm(o, g_ref[h]))
    o_ref[0] = jnp.concatenate(outs, axis=1)


def _attn_sample(q8, k8, v8, cache_k, cache_v, head_g, n_new, past):
    b = q8.shape[0]
    lw = cache_k.shape[1]
    assert lw == past and lw % 16 == 0 and lw >= WINDOWS[2] and n_new <= 4
    ck_tail = cache_k.reshape(b, lw, ATT_W)
    cv_tail = cache_v.reshape(b, lw, ATT_W)
    ck_dil = cache_k.reshape(b, lw // 16, 16 * ATT_W)
    cv_dil = cache_v.reshape(b, lw // 16, 16 * ATT_W)
    new_spec = pl.BlockSpec((1, S_PAD, ATT_W), lambda i: (i, 0, 0))
    tail_spec = pl.BlockSpec((1, TAIL, ATT_W), lambda i: (i, lw // TAIL - 1, 0))
    dil_spec = pl.BlockSpec((1, lw // 16, n_new * ATT_W), lambda i: (i, 0, 0))
    return pl.pallas_call(
        functools.partial(_attn_sample_kernel, n_new=n_new, past=past),
        grid=(b,),
        in_specs=[new_spec, new_spec, new_spec, tail_spec, tail_spec, dil_spec, dil_spec,
                  pl.BlockSpec((N_HEADS, 1, HEAD_DIM), lambda i: (0, 0, 0))],
        out_specs=new_spec,
        out_shape=jax.ShapeDtypeStruct((b, S_PAD, ATT_W), F32),
        compiler_params=_params(("arbitrary",), 48),
        name="attn_sample",
    )(q8, k8, v8, ck_tail, cv_tail, ck_dil, cv_dil, head_g.reshape(N_HEADS, 1, HEAD_DIM))


def _finish_kernel(x_ref, conv_ref, attn_ref, g1_ref, sh2_ref, sc2_ref, lng_ref, lnb_ref, n2g_ref,
                   wout_ref, wq_ref, x1_ref, h2_ref, qp_ref, cat_scr):
    y = conv_ref[...]
    mu = jnp.mean(y, axis=-1, keepdims=True)
    yc = y - mu
    var = jnp.mean(yc * yc, axis=-1, keepdims=True)
    z = yc * lax.rsqrt(var + EPS) * lng_ref[...] + lnb_ref[...]
    cat_scr[:, :CONV_CH] = (z * jax.nn.sigmoid(z)).astype(BF16)
    cat_scr[:, CONV_CH:] = attn_ref[...].astype(BF16)
    mix = jnp.dot(cat_scr[...], wout_ref[...], preferred_element_type=F32)
    x1 = x_ref[...] + (1.0 + _rows(g1_ref)) * mix
    x1_ref[...] = x1
    n = x1 * lax.rsqrt(jnp.mean(x1 * x1, axis=-1, keepdims=True) + EPS)
    h2 = ((n * n2g_ref[...]) * (1.0 + _rows(sc2_ref)) + _rows(sh2_ref)).astype(BF16)
    h2_ref[...] = h2
    qp_ref[...] = jnp.dot(h2, wq_ref[...], preferred_element_type=F32).astype(BF16)


def _finish(x, conv_raw, attn, mods, group_rows, ln_g, ln_b, n2_g, wout_bf, wq_bf, tm):
    t = x.shape[0]
    row = lambda n: pl.BlockSpec((1, n), lambda i: (0, 0))
    const = lambda shape: pl.BlockSpec(shape, lambda i: (0, 0), pipeline_mode=pl.Buffered(1))
    tile = lambda n: pl.BlockSpec((tm, n), lambda i: (i, 0))
    return pl.pallas_call(
        _finish_kernel,
        grid=(t // tm,),
        in_specs=[tile(D_MODEL), tile(CONV_CH), tile(ATT_W),
                  _mod_spec(mods, 2, tm, group_rows, 1),
                  _mod_spec(mods, 3, tm, group_rows, 1),
                  _mod_spec(mods, 4, tm, group_rows, 1),
                  row(CONV_CH), row(CONV_CH), row(D_MODEL),
                  const((D_MODEL, D_MODEL)), const((D_MODEL, D_MODEL))],
        out_specs=[tile(D_MODEL), tile(D_MODEL), tile(D_MODEL)],
        out_shape=[jax.ShapeDtypeStruct((t, D_MODEL), F32),
                   jax.ShapeDtypeStruct((t, D_MODEL), BF16),
                   jax.ShapeDtypeStruct((t, D_MODEL), BF16)],
        scratch_shapes=[pltpu.VMEM((tm, D_MODEL), BF16)],
        compiler_params=_params(("arbitrary",), 56),
        name="finish",
    )(x, conv_raw, attn, mods, mods, mods, ln_g.reshape(1, CONV_CH), ln_b.reshape(1, CONV_CH),
      n2_g.reshape(1, D_MODEL), wout_bf, wq_bf)


def _top_values(s, k, out_scr):
    for i in range(k):
        m = jnp.max(s, axis=0, keepdims=True)
        out_scr[i:i + 1, :] = m
        s = jnp.where(s == m, NEG_INF, s)


def _topk_kernel(qp_ref, keys_ref, s1_ref, s2_ref, e1_ref, e2_ref, thr_ref, a_scr, b_scr):
    tb = qp_ref.shape[0]
    for h in range(PEER_HEADS):
        sc = []
        for p in range(2):
            col = (h * 2 + p) * PEER_DK
            sc.append(_nt_dot(keys_ref[h, p], qp_ref[:, col:col + PEER_DK]))
        s1, s2 = sc
        _top_values(s1, PEER_TOPK, a_scr)
        _top_values(s2, PEER_TOPK, b_scr)
        b_top = b_scr[...]
        ridx = lax.broadcasted_iota(jnp.int32, (PEER_TOPK, tb), 0)
        cands = []
        for i in range(PEER_TOPK):
            cands.append(jnp.where(ridx < PEER_TOPK // (i + 1), a_scr[i:i + 1, :] + b_top, NEG_INF))
        big = jnp.int32(PEER_TOPK * PEER_TOPK)
        cidx = [ridx + i * PEER_TOPK for i in range(PEER_TOPK)]
        top = None
        z = jnp.zeros((1, tb), F32)
        m = None
        for _ in range(PEER_TOPK):
            m = jnp.max(functools.reduce(jnp.maximum, cands), axis=0, keepdims=True)
            top = m if top is None else top
            z = z + jnp.exp(m - top)
            hit = jnp.min(functools.reduce(jnp.minimum, [jnp.where(c == m, ci, big) for c, ci in zip(cands, cidx)]),
                          axis=0, keepdims=True)
            cands = [jnp.where(ci == hit, NEG_INF, c) for c, ci in zip(cands, cidx)]
        s1_ref[h] = s1
        s2_ref[h] = s2
        e1_ref[h] = jnp.exp(s1 - a_scr[0:1, :])
        e2_ref[h] = jnp.exp(s2 - b_scr[0:1, :]) / z
        thr_ref[h] = m


def _topk(qp, keys_bf, tb):
    t = qp.shape[0]
    big = pl.BlockSpec((PEER_HEADS, PEER_NKEYS, tb), lambda i: (0, 0, i))
    big_sds = jax.ShapeDtypeStruct((PEER_HEADS, PEER_NKEYS, t), F32)
    return pl.pallas_call(
        _topk_kernel,
        grid=(t // tb,),
        in_specs=[pl.BlockSpec((tb, D_MODEL), lambda i: (i, 0)),
                  pl.BlockSpec((PEER_HEADS, 2, PEER_NKEYS, PEER_DK), lambda i: (0, 0, 0, 0))],
        out_specs=[big, big, big, big, pl.BlockSpec((PEER_HEADS, 1, tb), lambda i: (0, 0, i))],
        out_shape=[big_sds, big_sds, big_sds, big_sds, jax.ShapeDtypeStruct((PEER_HEADS, 1, t), F32)],
        scratch_shapes=[pltpu.VMEM((PEER_TOPK, tb), F32), pltpu.VMEM((PEER_TOPK, tb), F32)],
        compiler_params=_params(("arbitrary",), 32),
        name="peer_topk",
    )(qp, keys_bf)


_SQRT_HALF = float(np.sqrt(0.5))


def _gelu(x):
    return 0.5 * x * (1.0 + lax.erf(x * _SQRT_HALF))


def _peer_kernel(h_ref, u_ref, vt_ref, s1_ref, s2_ref, e1_ref, e2_ref, thr_ref, o_ref,
                 acc_scr, ht_scr, w_scr):
    j = pl.program_id(1)
    ec, tb = ht_scr.shape

    @pl.when(j == 0)
    def _():
        acc_scr[...] = jnp.zeros_like(acc_scr)

    ht_scr[...] = _nt_dot(u_ref[...], h_ref[...])
    n_i1 = ec // PEER_NKEYS
    assert SUBLANES % n_i1 == 0
    tile0 = pl.multiple_of((j * n_i1) // SUBLANES * SUBLANES, SUBLANES)
    off = (j * n_i1) % SUBLANES

    def row_of(ref, h, ii, sl):
        tile = ref[h, pl.ds(tile0, SUBLANES), sl]
        picked = tile[ii:ii + 1, :]
        for c in range(1, SUBLANES // n_i1):
            picked = jnp.where(off == c * n_i1, tile[c * n_i1 + ii:c * n_i1 + ii + 1, :], picked)
        return picked

    for ii in range(n_i1):
        for lc in range(tb // LANES):
            sl = slice(lc * LANES, (lc + 1) * LANES)
            gate = jnp.zeros((PEER_NKEYS, LANES), F32)
            for h in range(PEER_HEADS):
                pair = s2_ref[h, :, sl] + row_of(s1_ref, h, ii, sl)
                val = e2_ref[h, :, sl] * row_of(e1_ref, h, ii, sl)
                gate = gate + jnp.where(pair >= thr_ref[h, :, sl], val, 0.0)
            hid = ht_scr[ii * PEER_NKEYS:(ii + 1) * PEER_NKEYS, sl]
            w_scr[ii * PEER_NKEYS:(ii + 1) * PEER_NKEYS, sl] = (gate * _gelu(hid)).astype(BF16)
    acc_scr[...] += jnp.dot(vt_ref[...], w_scr[...], preferred_element_type=F32)

    @pl.when(j == pl.num_programs(1) - 1)
    def _():
        o_ref[...] = acc_scr[...].T


def _peer(h2, u_bf, vt_bf, s1, s2, e1, e2, thr, tb, ec):
    t = h2.shape[0]
    n_exp = u_bf.shape[0]
    big = pl.BlockSpec((PEER_HEADS, PEER_NKEYS, tb), lambda i, j: (0, 0, i))
    return pl.pallas_call(
        _peer_kernel,
        grid=(t // tb, n_exp // ec),
        in_specs=[pl.BlockSpec((tb, D_MODEL), lambda i, j: (i, 0)),
                  pl.BlockSpec((ec, D_MODEL), lambda i, j: (j, 0)),
                  pl.BlockSpec((D_MODEL, ec), lambda i, j: (0, j)),
                  big, big, big, big,
                  pl.BlockSpec((PEER_HEADS, 1, tb), lambda i, j: (0, 0, i))],
        out_specs=pl.BlockSpec((tb, D_MODEL), lambda i, j: (i, 0)),
        out_shape=jax.ShapeDtypeStruct((t, D_MODEL), F32),
        scratch_shapes=[pltpu.VMEM((D_MODEL, tb), F32), pltpu.VMEM((ec, tb), F32), pltpu.VMEM((ec, tb), BF16)],
        compiler_params=_params(("arbitrary", "arbitrary"), 56),
        name="peer",
    )(h2, u_bf, vt_bf, s1, s2, e1, e2, thr)


def _final_kernel(x1_ref, p_ref, g2_ref, fg_ref, y_ref):
    x2 = x1_ref[...] + (1.0 + _rows(g2_ref)) * p_ref[...]
    y_ref[...] = x2 * lax.rsqrt(jnp.mean(x2 * x2, axis=-1, keepdims=True) + EPS) * fg_ref[...]


def _final(x1, peer_out, mods, group_rows, final_g, tm):
    t = x1.shape[0]
    tile = pl.BlockSpec((tm, D_MODEL), lambda i: (i, 0))
    return pl.pallas_call(
        _final_kernel,
        grid=(t // tm,),
        in_specs=[tile, tile, _mod_spec(mods, 5, tm, group_rows, 1),
                  pl.BlockSpec((1, D_MODEL), lambda i: (0, 0))],
        out_specs=tile,
        out_shape=jax.ShapeDtypeStruct((t, D_MODEL), F32),
        compiler_params=_params(("arbitrary",), 48),
        name="final",
    )(x1, peer_out, mods, final_g.reshape(1, D_MODEL))


PAST_LEN = 2048
MIXIN_TOKENS = 512
TOKEN_TILE = 256
PEER_TOKENS = 512
PEER_CHUNK = 512
TOPK_TOKENS = 256


def _channel_tail(x, conv_raw, attn, mods, group_rows, weights):
    x1, h2, qp = _finish(x, conv_raw, attn, mods, group_rows, weights["ln_g"], weights["ln_b"],
                         weights["n2_g"], weights["wout"], weights["wq"], TOKEN_TILE)
    s1, s2, e1, e2, thr = _topk(qp, weights["keys"], TOPK_TOKENS)
    peer_out = _peer(h2, weights["u"], weights["vt"], s1, s2, e1, e2, thr, PEER_TOKENS, PEER_CHUNK)
    return _final(x1, peer_out, mods, group_rows, weights["final_g"], TOKEN_TILE)


def kernel(x_prompt, x_sample, cache_k, cache_v, state_conv, c_prompt, c_sample, w_ada, b_ada, norm1_g, w_in,
           conv_w, conv_b, conv_ln_g, conv_ln_b, attn_out_g, w_out, norm2_g, peer_wq, peer_keys, peer_u, peer_v,
           final_g):
    b, seq, d = x_prompt.shape
    db, n_new, _ = x_sample.shape
    depth = w_ada.shape[0]
    assert depth == 1 and d == D_MODEL
    cache_len = cache_k.shape[2]
    n_sample = db * n_new
    assert seq % MIXIN_TOKENS == 0 and n_sample % PEER_TOKENS == 0

    c_rows = jnp.concatenate([c_prompt, jnp.repeat(c_sample, n_new, axis=0)], axis=0)
    mods = _ada(c_rows, w_ada[0], b_ada[0])
    mods_p = mods[:b].reshape(b, 1, N_ADA * D_MODEL)
    mods_s = mods[b:]

    weights = {
        "ln_g": conv_ln_g[0], "ln_b": conv_ln_b[0], "n2_g": norm2_g[0],
        "wout": w_out[0].astype(BF16), "wq": peer_wq[0].astype(BF16),
        "keys": peer_keys[0].astype(BF16), "u": peer_u[0].astype(BF16),
        "vt": peer_v[0].T.astype(BF16), "final_g": final_g,
    }
    w_in_bf = w_in[0].astype(BF16)

    xp = x_prompt.reshape(b * seq, d)
    tables_p = _rope_tables(np.arange(seq))
    glu, q, k, v = _mixin(xp, mods_p, seq, norm1_g[0], w_in_bf, tables_p, MIXIN_TOKENS)
    glu3 = glu.reshape(b, seq, CONV_CH)
    conv_raw = _conv_prompt(glu3, jnp.zeros((b, CONV_PAD, CONV_CH), F32), conv_w[0], conv_b[0])
    attn = _attn_prompt(q.reshape(b, seq, ATT_W), k.reshape(b, seq, ATT_W), v.reshape(b, seq, ATT_W),
                        attn_out_g[0])
    y_prompt = _channel_tail(xp, conv_raw.reshape(b * seq, CONV_CH), attn.reshape(b * seq, ATT_W),
                             mods_p, seq, weights).reshape(b, seq, d)
    keep = min(WINDOWS[2], seq)
    new_k_prompt = k.reshape(b, seq, N_HEADS, HEAD_DIM)[:, seq - keep:][None]
    new_v_prompt = v.reshape(b, seq, N_HEADS, HEAD_DIM)[:, seq - keep:][None]
    new_conv_prompt = glu3[:, seq - CONV_HIST:][None]

    xs = x_sample.reshape(n_sample, d)
    tables_s = tuple(jnp.tile(t, (TOKEN_TILE // n_new, 1)) for t in _rope_tables(PAST_LEN + np.arange(n_new)))
    glu_s, q_s, k_s, v_s = _mixin(xs, mods_s, None, norm1_g[0], w_in_bf, tables_s, TOKEN_TILE)
    glu_s3 = glu_s.reshape(db, n_new, CONV_CH)
    conv_raw_s = _conv_sample(state_conv[0], glu_s3, conv_w[0], conv_b[0])
    pad8 = lambda a: jnp.pad(a.reshape(db, n_new, ATT_W), ((0, 0), (0, S_PAD - n_new), (0, 0)))
    attn_s = _attn_sample(pad8(q_s), pad8(k_s), pad8(v_s), cache_k[0], cache_v[0], attn_out_g[0], n_new,
                          cache_len)
    attn_s = attn_s[:, :n_new].reshape(n_sample, ATT_W)
    y_sample = _channel_tail(xs, conv_raw_s.reshape(n_sample, CONV_CH), attn_s,
                             mods_s, None, weights).reshape(db, n_new, d)
    new_k_sample = k_s.reshape(db, n_new, N_HEADS, HEAD_DIM)[None]
    new_v_sample = v_s.reshape(db, n_new, N_HEADS, HEAD_DIM)[None]
    new_conv_sample = jnp.concatenate([state_conv[0], glu_s3], axis=1)[:, n_new:][None]

    return (y_prompt, y_sample, new_k_prompt, new_v_prompt, new_conv_prompt,
            new_k_sample, new_v_sample, new_conv_sample)
```

```python
import functools

import numpy as np
import jax
import jax.numpy as jnp
from jax import lax
from jax.experimental import pallas as pl
from jax.experimental.pallas import tpu as pltpu

F32 = jnp.float32
BF16 = jnp.bfloat16

D_MODEL = 2048
CONV_CH = 1024
CONV_W = 31
N_HEADS = 8
HEAD_DIM = 128
ATT_W = N_HEADS * HEAD_DIM
ROT_DIM = HEAD_DIM // 4
ROPE_THETA = 500000.0
WINDOWS = (128, 512, 2048)
DILATIONS = (1, 4, 16)
Q_BLOCK = 128
ATT_SCALE = HEAD_DIM ** -0.5
N_ADA = 6
PEER_HEADS = 8
PEER_NKEYS = 128
PEER_DK = 128
PEER_TOPK = 16
EPS = 1e-6

LANES = 128
SUBLANES = 8
MIB = 1024 * 1024

NEG_INF = float("-inf")


def _nt_dot(a, b):
    return lax.dot_general(a, b, (((1,), (1,)), ((), ())), preferred_element_type=F32)


def _rows(ref):
    return ref[0] if len(ref.shape) == 3 else ref[...]


def _params(sem, vmem_mib):
    return pltpu.CompilerParams(dimension_semantics=sem, vmem_limit_bytes=vmem_mib * MIB)


def _ada_kernel(c_ref, w_ref, b_ref, o_ref):
    c = c_ref[...]
    a = (c * jax.nn.sigmoid(c)).astype(BF16)
    o_ref[...] = jnp.dot(a, w_ref[...].astype(BF16), preferred_element_type=F32) + b_ref[...]


def _ada(c_rows, w_ada, b_ada):
    rows = c_rows.shape[0]
    n = w_ada.shape[1]
    tn = 1024
    return pl.pallas_call(
        _ada_kernel,
        grid=(n // tn,),
        in_specs=[pl.BlockSpec((rows, D_MODEL), lambda j: (0, 0)),
                  pl.BlockSpec((D_MODEL, tn), lambda j: (0, j)),
                  pl.BlockSpec((1, tn), lambda j: (0, j))],
        out_specs=pl.BlockSpec((rows, tn), lambda j: (0, j)),
        out_shape=jax.ShapeDtypeStruct((rows, n), F32),
        compiler_params=_params(("arbitrary",), 48),
        name="ada",
    )(c_rows, w_ada, b_ada.reshape(1, n))


def _rope_heads(z, cos, sin_lo, sin_hi):
    outs = []
    for h in range(N_HEADS):
        zh = z[:, h * HEAD_DIM:(h + 1) * HEAD_DIM]
        up = pltpu.roll(zh, HEAD_DIM - ROT_DIM // 2, 1)
        dn = pltpu.roll(zh, ROT_DIM // 2, 1)
        outs.append(zh * cos + up * sin_lo + dn * sin_hi)
    return jnp.concatenate(outs, axis=1)


def _mixin_kernel(x_ref, sh_ref, sc_ref, g_ref, w_ref, cos_ref, slo_ref, shi_ref,
                  glu_ref, q_ref, k_ref, v_ref, h_scr, a_scr):
    j = pl.program_id(1)

    @pl.when(j == 0)
    def _():
        x = x_ref[...]
        y = x * lax.rsqrt(jnp.mean(x * x, axis=-1, keepdims=True) + EPS)
        h = (y * g_ref[...]) * (1.0 + _rows(sc_ref)) + _rows(sh_ref)
        h_scr[...] = h.astype(BF16)

    z = jnp.dot(h_scr[...], w_ref[...], preferred_element_type=F32)

    @pl.when(j == 0)
    def _():
        a_scr[...] = z

    @pl.when(j == 1)
    def _():
        glu_ref[...] = a_scr[...] * jax.nn.sigmoid(z)

    @pl.when(j == 2)
    def _():
        q_ref[...] = _rope_heads(z, cos_ref[...], slo_ref[...], shi_ref[...])

    @pl.when(j == 3)
    def _():
        k_ref[...] = _rope_heads(z, cos_ref[...], slo_ref[...], shi_ref[...])

    @pl.when(j == 4)
    def _():
        v_ref[...] = z


def _rope_tables(pos):
    inv = ROPE_THETA ** (-np.arange(0, ROT_DIM, 2, dtype=np.float32) / ROT_DIM)
    ang = (pos.astype(np.float32)[:, None] * inv[None, :]).astype(np.float32)
    cos, sin = np.cos(ang), np.sin(ang)
    half = ROT_DIM // 2
    n = pos.shape[0]
    c = np.ones((n, HEAD_DIM), np.float32)
    c[:, :half] = cos
    c[:, half:ROT_DIM] = cos
    s_lo = np.zeros((n, HEAD_DIM), np.float32)
    s_lo[:, :half] = -sin
    s_hi = np.zeros((n, HEAD_DIM), np.float32)
    s_hi[:, half:ROT_DIM] = sin
    return jnp.asarray(c), jnp.asarray(s_lo), jnp.asarray(s_hi)


def _mod_spec(mods, col, tm, group_rows, ngrid):
    if mods.ndim == 3:
        assert group_rows % tm == 0
        per_group = group_rows // tm
        if ngrid == 1:
            return pl.BlockSpec((1, 1, D_MODEL), lambda i: (i // per_group, 0, col))
        return pl.BlockSpec((1, 1, D_MODEL), lambda i, j: (i // per_group, 0, col))
    if ngrid == 1:
        return pl.BlockSpec((tm, D_MODEL), lambda i: (i, col))
    return pl.BlockSpec((tm, D_MODEL), lambda i, j: (i, col))


def _mixin(x, mods, group_rows, norm_g, w_in_bf, tables, tm):
    t = x.shape[0]
    cos, s_lo, s_hi = tables
    pos_blocks = cos.shape[0] // tm
    tab_spec = pl.BlockSpec((tm, HEAD_DIM), lambda i, j: (i % pos_blocks, 0))
    out_spec = pl.BlockSpec((tm, ATT_W), lambda i, j: (i, 0))
    out_sds = jax.ShapeDtypeStruct((t, ATT_W), F32)
    return pl.pallas_call(
        _mixin_kernel,
        grid=(t // tm, 5),
        in_specs=[pl.BlockSpec((tm, D_MODEL), lambda i, j: (i, 0)),
                  _mod_spec(mods, 0, tm, group_rows, 2),
                  _mod_spec(mods, 1, tm, group_rows, 2),
                  pl.BlockSpec((1, D_MODEL), lambda i, j: (0, 0)),
                  pl.BlockSpec((D_MODEL, ATT_W), lambda i, j: (0, j)),
                  tab_spec, tab_spec, tab_spec],
        out_specs=[out_spec, out_spec, out_spec, out_spec],
        out_shape=[out_sds, out_sds, out_sds, out_sds],
        scratch_shapes=[pltpu.VMEM((tm, D_MODEL), BF16), pltpu.VMEM((tm, ATT_W), F32)],
        compiler_params=_params(("arbitrary", "arbitrary"), 56),
        name="mixin",
    )(x, mods, mods, norm_g.reshape(1, D_MODEL), w_in_bf, cos, s_lo, s_hi)


CONV_HIST = CONV_W - 1
CONV_PAD = 32
CONV_ROWS = 64


def _conv_prompt_kernel(g_ref, hist_ref, w_ref, b_ref, o_ref, xp_scr):
    seq = g_ref.shape[1]
    xp_scr[0:CONV_PAD, :] = hist_ref[0]
    xp_scr[CONV_PAD:CONV_PAD + seq, :] = g_ref[0]
    lead = CONV_PAD - CONV_HIST

    def body(c, carry):
        t0 = pl.multiple_of(c * CONV_ROWS, CONV_ROWS)
        win = xp_scr[pl.ds(t0, CONV_ROWS + CONV_PAD), :]
        acc = jnp.zeros((CONV_ROWS, LANES), F32) + b_ref[...]
        for sub in range(SUBLANES):
            shifted = win if sub == 0 else pltpu.roll(win, CONV_ROWS + CONV_PAD - sub, 0)
            for a in range(CONV_PAD // SUBLANES + 1):
                off = a * SUBLANES + sub
                j = off - lead
                if 0 <= j < CONV_W:
                    acc = acc + shifted[a * SUBLANES:a * SUBLANES + CONV_ROWS, :] * w_ref[j:j + 1, :]
        o_ref[0, pl.ds(t0, CONV_ROWS), :] = acc
        return carry

    lax.fori_loop(0, seq // CONV_ROWS, body, 0)


def _conv_prompt(glu3, hist_pad, conv_w, conv_b):
    b, seq, ch = glu3.shape
    return pl.pallas_call(
        _conv_prompt_kernel,
        grid=(b, ch // LANES),
        in_specs=[pl.BlockSpec((1, seq, LANES), lambda i, c: (i, 0, c)),
                  pl.BlockSpec((1, CONV_PAD, LANES), lambda i, c: (i, 0, c)),
                  pl.BlockSpec((CONV_W, LANES), lambda i, c: (0, c)),
                  pl.BlockSpec((1, LANES), lambda i, c: (0, c))],
        out_specs=pl.BlockSpec((1, seq, LANES), lambda i, c: (i, 0, c)),
        out_shape=jax.ShapeDtypeStruct((b, seq, ch), F32),
        scratch_shapes=[pltpu.VMEM((seq + CONV_PAD, LANES), F32)],
        compiler_params=_params(("arbitrary", "arbitrary"), 32),
        name="conv_prompt",
    )(glu3, hist_pad, conv_w, conv_b.reshape(1, ch))


def _conv_sample_kernel(hist_ref, g_ref, wh_ref, wn_ref, b_ref, o_ref):
    hist = hist_ref[...]
    g = g_ref[...]
    n_new = g.shape[1]
    for t in range(n_new):
        y = jnp.sum(hist * wh_ref[t][None], axis=1) + jnp.sum(g * wn_ref[t][None], axis=1)
        o_ref[t] = y + b_ref[...]


def _conv_sample(state, glu3, conv_w, conv_b):
    b, n_new, ch = glu3.shape
    wh = jnp.stack([jnp.pad(conv_w[:CONV_HIST - t], ((t, 0), (0, 0))) for t in range(n_new)])
    wn = jnp.stack([jnp.pad(conv_w[CONV_HIST - t:], ((0, n_new - 1 - t), (0, 0))) for t in range(n_new)])
    nb = 32
    return pl.pallas_call(
        _conv_sample_kernel,
        grid=(b // nb, ch // LANES),
        in_specs=[pl.BlockSpec((nb, CONV_HIST, LANES), lambda i, c: (i, 0, c)),
                  pl.BlockSpec((nb, n_new, LANES), lambda i, c: (i, 0, c)),
                  pl.BlockSpec((n_new, CONV_HIST, LANES), lambda i, c: (0, 0, c)),
                  pl.BlockSpec((n_new, n_new, LANES), lambda i, c: (0, 0, c)),
                  pl.BlockSpec((1, LANES), lambda i, c: (0, c))],
        out_specs=pl.BlockSpec((n_new, nb, LANES), lambda i, c: (0, i, c)),
        out_shape=jax.ShapeDtypeStruct((n_new, b, ch), F32),
        compiler_params=_params(("arbitrary", "arbitrary"), 32),
        name="conv_sample",
    )(state, glu3, wh, wn, conv_b.reshape(1, ch)).transpose(1, 0, 2)


ATTN_UNROLL = 4
MERGE_ROWS = 256


def _head_norm(o, g):
    return o * lax.rsqrt(jnp.mean(o * o, axis=-1, keepdims=True) + EPS) * g


def _attn_prompt_kernel(q_ref, k_ref, v_ref, g_ref, o_ref, acc_scr, m_scr, l_scr):
    seq = q_ref.shape[1]
    qi = lax.broadcasted_iota(jnp.int32, (Q_BLOCK, Q_BLOCK), 0)
    kj = lax.broadcasted_iota(jnp.int32, (Q_BLOCK, Q_BLOCK), 1)
    cur_ok = kj <= qi

    for pat, (win, dil) in enumerate(zip(WINDOWS, DILATIONS)):
        assert win // dil == Q_BLOCK
        n_blk = seq // dil // Q_BLOCK

        def rows(ref, start, dil=dil):
            if dil == 1:
                return ref[0, pl.ds(start, Q_BLOCK), :]
            return ref[0, pl.ds(start, Q_BLOCK, stride=dil), :]

        def put(ref, start, val, dil=dil, pat=pat):
            if dil == 1:
                ref[pat, pl.ds(start, Q_BLOCK), :] = val
            else:
                ref[pat, pl.ds(start, Q_BLOCK, stride=dil), :] = val

        def body(i, carry, dil=dil, n_blk=n_blk, rows=rows, put=put):
            r = i // n_blk
            n = i % n_blk
            cur = r + dil * Q_BLOCK * n
            prev = r + dil * Q_BLOCK * jnp.maximum(n - 1, 0)
            if dil == 1:
                cur = pl.multiple_of(cur, Q_BLOCK)
                prev = pl.multiple_of(prev, Q_BLOCK)
            qb = rows(q_ref, cur).astype(BF16)
            kc = rows(k_ref, cur).astype(BF16)
            vc = rows(v_ref, cur).astype(BF16)
            kp = rows(k_ref, prev).astype(BF16)
            vp = rows(v_ref, prev).astype(BF16)
            s_c = jnp.where(cur_ok, _nt_dot(qb, kc) * ATT_SCALE, NEG_INF)
            no_prev = jnp.where(n > 0, 0, Q_BLOCK)
            s_p = jnp.where(kj >= qi + no_prev, _nt_dot(qb, kp) * ATT_SCALE, NEG_INF)
            m = jnp.maximum(jnp.max(s_c, axis=-1, keepdims=True), jnp.max(s_p, axis=-1, keepdims=True))
            p_c = jnp.exp(s_c - m)
            p_p = jnp.exp(s_p - m)
            l = jnp.sum(p_c, axis=-1, keepdims=True) + jnp.sum(p_p, axis=-1, keepdims=True)
            acc = (jnp.dot(p_c.astype(BF16), vc, preferred_element_type=F32)
                   + jnp.dot(p_p.astype(BF16), vp, preferred_element_type=F32))
            put(acc_scr, cur, acc)
            put(m_scr, cur, jnp.broadcast_to(m, (Q_BLOCK, HEAD_DIM)))
            put(l_scr, cur, jnp.broadcast_to(l, (Q_BLOCK, HEAD_DIM)))
            return carry

        lax.fori_loop(0, dil * n_blk, body, 0, unroll=ATTN_UNROLL)

    n_pat = len(WINDOWS)

    def merge(c, carry):
        sl = pl.ds(pl.multiple_of(c * MERGE_ROWS, MERGE_ROWS), MERGE_ROWS)
        ms = [m_scr[p, sl, :] for p in range(n_pat)]
        top = functools.reduce(jnp.maximum, ms)
        ws = [jnp.exp(m - top) for m in ms]
        num = functools.reduce(jnp.add, [w * acc_scr[p, sl, :] for p, w in enumerate(ws)])
        den = functools.reduce(jnp.add, [w * l_scr[p, sl, :] for p, w in enumerate(ws)])
        o_ref[0, sl, :] = _head_norm(num / den, g_ref[0])
        return carry

    lax.fori_loop(0, seq // MERGE_ROWS, merge, 0)


def _attn_prompt(q3, k3, v3, head_g):
    b, seq, _ = q3.shape
    spec = pl.BlockSpec((1, seq, HEAD_DIM), lambda i, h: (i, 0, h))
    return pl.pallas_call(
        _attn_prompt_kernel,
        grid=(b, N_HEADS),
        in_specs=[spec, spec, spec, pl.BlockSpec((1, 1, HEAD_DIM), lambda i, h: (h, 0, 0))],
        out_specs=spec,
        out_shape=jax.ShapeDtypeStruct((b, seq, ATT_W), F32),
        scratch_shapes=[pltpu.VMEM((len(WINDOWS), seq, HEAD_DIM), F32)] * 3,
        compiler_params=_params(("arbitrary", "arbitrary"), 32),
        name="attn_prompt",
    )(q3, k3, v3, head_g.reshape(N_HEADS, 1, HEAD_DIM))


S_PAD = 8
TAIL = 512
N_COLS = LANES


def _attn_sample_kernel(q_ref, kn_ref, vn_ref, kt_ref, vt_ref, kd_ref, vd_ref, g_ref, o_ref,
                        k_scr, v_scr, kd2, vd2, *, n_new, past):
    n_dil = past // DILATIONS[2]
    kd2[...] = kd_ref[0].reshape(n_dil * n_new * N_HEADS, HEAD_DIM)
    vd2[...] = vd_ref[0].reshape(n_dil * n_new * N_HEADS, HEAD_DIM)
    for h in range(N_HEADS):
        lanes = slice(h * HEAD_DIM, (h + 1) * HEAD_DIM)
        k_scr[0:TAIL, lanes] = kt_ref[0, pl.ds(h, TAIL, stride=N_HEADS), :].astype(BF16)
        v_scr[0:TAIL, lanes] = vt_ref[0, pl.ds(h, TAIL, stride=N_HEADS), :].astype(BF16)
        for res in range(n_new):
            dst = slice(TAIL + res * n_dil, TAIL + (res + 1) * n_dil)
            src = pl.ds(res * N_HEADS + h, n_dil, stride=n_new * N_HEADS)
            k_scr[dst, lanes] = kd2[src, :].astype(BF16)
            v_scr[dst, lanes] = vd2[src, :].astype(BF16)
    dil0 = TAIL

    q8 = q_ref[0]
    qt = jnp.concatenate([q8] * N_HEADS + [jnp.zeros_like(q8)] * (N_COLS // S_PAD - N_HEADS), axis=0)
    row = lax.broadcasted_iota(jnp.int32, (N_COLS, ATT_W), 0)
    lane = lax.broadcasted_iota(jnp.int32, (N_COLS, ATT_W), 1)
    qbd = jnp.where(row // S_PAD == lane // HEAD_DIM, qt, 0.0).astype(BF16)

    def col_query(shape):
        return lax.broadcasted_iota(jnp.int32, shape, 1) % S_PAD % n_new

    def key_row(shape):
        return lax.broadcasted_iota(jnp.int32, shape, 0)

    groups = []
    s = _nt_dot(k_scr[0:TAIL, :], qbd) * ATT_SCALE
    r, c = key_row(s.shape), col_query(s.shape)
    base = past - TAIL
    mult = ((((base + r) % DILATIONS[1] == (past + c) % DILATIONS[1])
             & (base + r >= past + c - WINDOWS[1])).astype(F32)
            + (base + r >= past + c - WINDOWS[0]).astype(F32))
    groups.append((s, mult, v_scr[0:TAIL, :]))
    for res in range(n_new):
        rows = slice(dil0 + res * n_dil, dil0 + (res + 1) * n_dil)
        s = _nt_dot(k_scr[rows, :], qbd) * ATT_SCALE
        mult = (col_query(s.shape) == res).astype(F32)
        groups.append((s, mult, v_scr[rows, :]))
    s = _nt_dot(kn_ref[0].astype(BF16), qbd) * ATT_SCALE
    r, c = key_row(s.shape), col_query(s.shape)
    mult = (r <= c).astype(F32) + 2.0 * (r == c).astype(F32)
    groups.append((s, mult, vn_ref[0].astype(BF16)))

    m = None
    for s, mult, _ in groups:
        gm = jnp.max(jnp.where(mult > 0, s, NEG_INF), axis=0, keepdims=True)
        m = gm if m is None else jnp.maximum(m, gm)
    num = jnp.zeros((N_COLS, ATT_W), F32)
    den = jnp.zeros((N_COLS, LANES), F32)
    for s, mult, v in groups:
        p = mult * jnp.exp(jnp.where(mult > 0, s - m, NEG_INF))
        pt = p.T.astype(BF16)
        num = num + jnp.dot(pt, v, preferred_element_type=F32)
        den = den + jnp.dot(pt, jnp.ones((v.shape[0], LANES), BF16), preferred_element_type=F32)
    outs = []
    for h in range(N_HEADS):
        o = (num[h * S_PAD:(h + 1) * S_PAD, h * HEAD_DIM:(h + 1) * HEAD_DIM]
             / den[h * S_PAD:(h + 1) * S_PAD, :])
        outs.append(_head_norm(o, g_ref[h]))
    o_ref[0] = jnp.concatenate(outs, axis=1)


def _attn_sample(q8, k8, v8, cache_k, cache_v, head_g, n_new, past):
    b = q8.shape[0]
    lw = cache_k.shape[1]
    assert lw == past and lw % 16 == 0 and lw >= WINDOWS[2] and n_new <= 4
    dil = DILATIONS[2]
    ck_rows = cache_k.reshape(b, lw * N_HEADS, HEAD_DIM)
    cv_rows = cache_v.reshape(b, lw * N_HEADS, HEAD_DIM)
    ck_grp = cache_k.reshape(b, lw // dil, dil * N_HEADS, HEAD_DIM)
    cv_grp = cache_v.reshape(b, lw // dil, dil * N_HEADS, HEAD_DIM)
    new_spec = pl.BlockSpec((1, S_PAD, ATT_W), lambda i: (i, 0, 0))
    tail_spec = pl.BlockSpec((1, TAIL * N_HEADS, HEAD_DIM), lambda i: (i, lw // TAIL - 1, 0))
    dil_spec = pl.BlockSpec((1, lw // dil, n_new * N_HEADS, HEAD_DIM), lambda i: (i, 0, 0, 0))
    n_keys = TAIL + n_new * (lw // dil)
    return pl.pallas_call(
        functools.partial(_attn_sample_kernel, n_new=n_new, past=past),
        grid=(b,),
        in_specs=[new_spec, new_spec, new_spec, tail_spec, tail_spec, dil_spec, dil_spec,
                  pl.BlockSpec((N_HEADS, 1, HEAD_DIM), lambda i: (0, 0, 0))],
        out_specs=new_spec,
        out_shape=jax.ShapeDtypeStruct((b, S_PAD, ATT_W), F32),
        scratch_shapes=[pltpu.VMEM((n_keys, ATT_W), BF16), pltpu.VMEM((n_keys, ATT_W), BF16),
                        pltpu.VMEM(((lw // dil) * n_new * N_HEADS, HEAD_DIM), F32),
                        pltpu.VMEM(((lw // dil) * n_new * N_HEADS, HEAD_DIM), F32)],
        compiler_params=_params(("arbitrary",), 48),
        name="attn_sample",
    )(q8, k8, v8, ck_rows, cv_rows, ck_grp, cv_grp, head_g.reshape(N_HEADS, 1, HEAD_DIM))


def _finish_kernel(x_ref, conv_ref, attn_ref, g1_ref, sh2_ref, sc2_ref, lng_ref, lnb_ref, n2g_ref,
                   wout_ref, wq_ref, x1_ref, h2t_ref, qp_ref, cat_scr):
    y = conv_ref[...]
    mu = jnp.mean(y, axis=-1, keepdims=True)
    yc = y - mu
    var = jnp.mean(yc * yc, axis=-1, keepdims=True)
    z = yc * lax.rsqrt(var + EPS) * lng_ref[...] + lnb_ref[...]
    cat_scr[:, :CONV_CH] = (z * jax.nn.sigmoid(z)).astype(BF16)
    cat_scr[:, CONV_CH:] = attn_ref[...].astype(BF16)
    mix = jnp.dot(cat_scr[...], wout_ref[...], preferred_element_type=F32)
    x1 = x_ref[...] + (1.0 + _rows(g1_ref)) * mix
    x1_ref[...] = x1
    n = x1 * lax.rsqrt(jnp.mean(x1 * x1, axis=-1, keepdims=True) + EPS)
    h2 = (n * n2g_ref[...]) * (1.0 + _rows(sc2_ref)) + _rows(sh2_ref)
    h2t_ref[...] = h2.T.astype(BF16)
    qp_ref[...] = jnp.dot(h2.astype(BF16), wq_ref[...], preferred_element_type=F32).astype(BF16)


def _finish(x, conv_raw, attn, mods, group_rows, ln_g, ln_b, n2_g, wout_bf, wq_bf, tm):
    t = x.shape[0]
    row = lambda n: pl.BlockSpec((1, n), lambda i: (0, 0))
    const = lambda shape: pl.BlockSpec(shape, lambda i: (0, 0), pipeline_mode=pl.Buffered(1))
    tile = lambda n: pl.BlockSpec((tm, n), lambda i: (i, 0))
    return pl.pallas_call(
        _finish_kernel,
        grid=(t // tm,),
        in_specs=[tile(D_MODEL), tile(CONV_CH), tile(ATT_W),
                  _mod_spec(mods, 2, tm, group_rows, 1),
                  _mod_spec(mods, 3, tm, group_rows, 1),
                  _mod_spec(mods, 4, tm, group_rows, 1),
                  row(CONV_CH), row(CONV_CH), row(D_MODEL),
                  const((D_MODEL, D_MODEL)), const((D_MODEL, D_MODEL))],
        out_specs=[tile(D_MODEL), pl.BlockSpec((D_MODEL, tm), lambda i: (0, i)), tile(D_MODEL)],
        out_shape=[jax.ShapeDtypeStruct((t, D_MODEL), F32),
                   jax.ShapeDtypeStruct((D_MODEL, t), BF16),
                   jax.ShapeDtypeStruct((t, D_MODEL), BF16)],
        scratch_shapes=[pltpu.VMEM((tm, D_MODEL), BF16)],
        compiler_params=_params(("arbitrary",), 56),
        name="finish",
    )(x, conv_raw, attn, mods, mods, mods, ln_g.reshape(1, CONV_CH), ln_b.reshape(1, CONV_CH),
      n2_g.reshape(1, D_MODEL), wout_bf, wq_bf)


TOPK_RANK = PEER_TOPK + 1
TOPK_ROWS = -(-TOPK_RANK // SUBLANES) * SUBLANES


def _top_values(s, k, out_scr):
    out_scr[...] = jnp.full(out_scr.shape, NEG_INF, F32)
    for i in range(k):
        m = jnp.max(s, axis=0, keepdims=True)
        out_scr[i:i + 1, :] = m
        s = jnp.where(s == m, NEG_INF, s)


def _pair_candidates(a_scr, b_scr):
    tb = a_scr.shape[1]
    rows_all = lax.broadcasted_iota(jnp.int32, (TOPK_ROWS, tb), 0)
    rows_one = lax.broadcasted_iota(jnp.int32, (SUBLANES, tb), 0)
    cands = [a_scr[0:1, :] + b_scr[...],
             jnp.where(rows_all >= 1, b_scr[0:1, :] + a_scr[...], NEG_INF)]
    single = []
    for i in range(1, TOPK_RANK):
        j_max = TOPK_RANK // (i + 1) - 1
        if j_max >= 2:
            assert j_max < SUBLANES
            ok = (rows_one >= 1) & (rows_one <= j_max)
            cands.append(jnp.where(ok, a_scr[i:i + 1, :] + b_scr[0:SUBLANES, :], NEG_INF))
        elif j_max == 1:
            single.append(i)
    if single:
        assert single == list(range(single[0], single[-1] + 1)) and single[-1] < SUBLANES
        ok = (rows_one >= single[0]) & (rows_one <= single[-1])
        cands.append(jnp.where(ok, b_scr[1:2, :] + a_scr[0:SUBLANES, :], NEG_INF))
    return cands


def _column_reduce(arrays, combine, reduce_rows):
    by_rows = {}
    for a in arrays:
        by_rows[a.shape[0]] = a if a.shape[0] not in by_rows else combine(by_rows[a.shape[0]], a)
    return functools.reduce(combine, [reduce_rows(a, axis=0, keepdims=True) for a in by_rows.values()])


def _topk_kernel(qp_ref, keys_ref, s1_ref, s2_ref, e1_ref, e2_ref, thr_ref, a_scr, b_scr):
    tb = qp_ref.shape[0]
    for h in range(PEER_HEADS):
        sc = []
        for p in range(2):
            col = (h * 2 + p) * PEER_DK
            sc.append(_nt_dot(keys_ref[h, p], qp_ref[:, col:col + PEER_DK]))
        s1, s2 = sc
        _top_values(s1, TOPK_RANK, a_scr)
        _top_values(s2, TOPK_RANK, b_scr)
        cands = _pair_candidates(a_scr, b_scr)
        top = None
        cum = jnp.zeros((1, tb), F32)
        z = jnp.zeros((1, tb), F32)
        v_in = jnp.zeros((1, tb), F32)
        v_out = jnp.zeros((1, tb), F32)
        for _ in range(TOPK_RANK):
            m = _column_reduce(cands, jnp.maximum, jnp.max)
            hits = [c == m for c in cands]
            cnt = _column_reduce([jnp.where(e, 1.0, 0.0) for e in hits], jnp.add, jnp.sum)
            cands = [jnp.where(e, NEG_INF, c) for e, c in zip(hits, cands)]
            top = m if top is None else top
            take = jnp.minimum(cnt, jnp.maximum(PEER_TOPK - cum, 0.0))
            z = z + take * jnp.exp(m - top)
            reached = cum + cnt
            v_in = jnp.where((cum < PEER_TOPK) & (reached >= PEER_TOPK), m, v_in)
            v_out = jnp.where((cum < TOPK_RANK) & (reached >= TOPK_RANK), m, v_out)
            cum = reached
        s1_ref[h] = s1
        s2_ref[h] = s2
        e1_ref[h] = jnp.exp(s1 - a_scr[0:1, :])
        e2_ref[h] = jnp.exp(s2 - b_scr[0:1, :]) / z
        thr_ref[h] = 0.5 * (v_in + v_out)


def _topk(qp, keys_bf, tb):
    t = qp.shape[0]
    big = pl.BlockSpec((PEER_HEADS, PEER_NKEYS, tb), lambda i: (0, 0, i))
    big_sds = jax.ShapeDtypeStruct((PEER_HEADS, PEER_NKEYS, t), F32)
    return pl.pallas_call(
        _topk_kernel,
        grid=(t // tb,),
        in_specs=[pl.BlockSpec((tb, D_MODEL), lambda i: (i, 0)),
                  pl.BlockSpec((PEER_HEADS, 2, PEER_NKEYS, PEER_DK), lambda i: (0, 0, 0, 0))],
        out_specs=[big, big, big, big, pl.BlockSpec((PEER_HEADS, 1, tb), lambda i: (0, 0, i))],
        out_shape=[big_sds, big_sds, big_sds, big_sds, jax.ShapeDtypeStruct((PEER_HEADS, 1, t), F32)],
        scratch_shapes=[pltpu.VMEM((TOPK_ROWS, tb), F32), pltpu.VMEM((TOPK_ROWS, tb), F32)],
        compiler_params=_params(("arbitrary",), 32),
        name="peer_topk",
    )(qp, keys_bf)


_SQRT_HALF = float(np.sqrt(0.5))


def _gelu(x):
    return 0.5 * x * (1.0 + lax.erf(x * _SQRT_HALF))


def _peer_kernel(ht_ref, u_ref, vt_ref, s1_ref, s2_ref, e1_ref, e2_ref, thr_ref, o_ref,
                 acc_scr, hid_scr, w_even, w_odd):
    j = pl.program_id(1)
    n_chunks = pl.num_programs(1) - 1
    ec, tb = hid_scr.shape

    @pl.when(j == 0)
    def _():
        acc_scr[...] = jnp.zeros_like(acc_scr)
        w_odd[...] = jnp.zeros_like(w_odd)

    @pl.when(j % 2 == 0)
    def _():
        _peer_step(j, n_chunks, ht_ref, u_ref, vt_ref, s1_ref, s2_ref, e1_ref, e2_ref, thr_ref,
                   acc_scr, hid_scr, w_odd, w_even)

    @pl.when(j % 2 == 1)
    def _():
        _peer_step(j, n_chunks, ht_ref, u_ref, vt_ref, s1_ref, s2_ref, e1_ref, e2_ref, thr_ref,
                   acc_scr, hid_scr, w_even, w_odd)

    @pl.when(j == n_chunks)
    def _():
        o_ref[...] = acc_scr[...].T


def _peer_step(j, n_chunks, ht_ref, u_ref, vt_ref, s1_ref, s2_ref, e1_ref, e2_ref, thr_ref,
               acc_scr, hid_scr, w_prev, w_next):
    ec, tb = hid_scr.shape
    acc_scr[...] += jnp.dot(vt_ref[...], w_prev[...], preferred_element_type=F32)
    hid_scr[...] = jnp.dot(u_ref[...], ht_ref[...], preferred_element_type=F32)
    jc = jnp.minimum(j, n_chunks - 1)
    n_i1 = ec // PEER_NKEYS
    assert SUBLANES % n_i1 == 0
    tile0 = pl.multiple_of((jc * n_i1) // SUBLANES * SUBLANES, SUBLANES)
    off = (jc * n_i1) % SUBLANES

    def row_of(ref, h, ii, sl):
        tile = ref[h, pl.ds(tile0, SUBLANES), sl]
        picked = tile[ii:ii + 1, :]
        for c in range(1, SUBLANES // n_i1):
            picked = jnp.where(off == c * n_i1, tile[c * n_i1 + ii:c * n_i1 + ii + 1, :], picked)
        return picked

    for ii in range(n_i1):
        for lc in range(tb // LANES):
            sl = slice(lc * LANES, (lc + 1) * LANES)
            gate = jnp.zeros((PEER_NKEYS, LANES), F32)
            for h in range(PEER_HEADS):
                bound = thr_ref[h, :, sl] - row_of(s1_ref, h, ii, sl)
                val = e2_ref[h, :, sl] * row_of(e1_ref, h, ii, sl)
                gate = gate + jnp.where(s2_ref[h, :, sl] >= bound, val, 0.0)
            hid = hid_scr[ii * PEER_NKEYS:(ii + 1) * PEER_NKEYS, sl]
            w_next[ii * PEER_NKEYS:(ii + 1) * PEER_NKEYS, sl] = (gate * _gelu(hid)).astype(BF16)


def _peer(h2t, u_bf, vt_bf, s1, s2, e1, e2, thr, tb, ec):
    t = h2t.shape[1]
    n_chunks = u_bf.shape[0] // ec
    big = pl.BlockSpec((PEER_HEADS, PEER_NKEYS, tb), lambda i, j: (0, 0, i))
    return pl.pallas_call(
        _peer_kernel,
        grid=(t // tb, n_chunks + 1),
        in_specs=[pl.BlockSpec((D_MODEL, tb), lambda i, j: (0, i)),
                  pl.BlockSpec((ec, D_MODEL), lambda i, j: (jnp.minimum(j, n_chunks - 1), 0)),
                  pl.BlockSpec((D_MODEL, ec), lambda i, j: (0, jnp.maximum(j - 1, 0))),
                  big, big, big, big,
                  pl.BlockSpec((PEER_HEADS, 1, tb), lambda i, j: (0, 0, i))],
        out_specs=pl.BlockSpec((tb, D_MODEL), lambda i, j: (i, 0)),
        out_shape=jax.ShapeDtypeStruct((t, D_MODEL), F32),
        scratch_shapes=[pltpu.VMEM((D_MODEL, tb), F32), pltpu.VMEM((ec, tb), F32),
                        pltpu.VMEM((ec, tb), BF16), pltpu.VMEM((ec, tb), BF16)],
        compiler_params=_params(("arbitrary", "arbitrary"), 56),
        name="peer",
    )(h2t, u_bf, vt_bf, s1, s2, e1, e2, thr)


def _final_kernel(x1_ref, p_ref, g2_ref, fg_ref, y_ref):
    x2 = x1_ref[...] + (1.0 + _rows(g2_ref)) * p_ref[...]
    y_ref[...] = x2 * lax.rsqrt(jnp.mean(x2 * x2, axis=-1, keepdims=True) + EPS) * fg_ref[...]


def _final(x1, peer_out, mods, group_rows, final_g, tm):
    t = x1.shape[0]
    tile = pl.BlockSpec((tm, D_MODEL), lambda i: (i, 0))
    return pl.pallas_call(
        _final_kernel,
        grid=(t // tm,),
        in_specs=[tile, tile, _mod_spec(mods, 5, tm, group_rows, 1),
                  pl.BlockSpec((1, D_MODEL), lambda i: (0, 0))],
        out_specs=tile,
        out_shape=jax.ShapeDtypeStruct((t, D_MODEL), F32),
        compiler_params=_params(("arbitrary",), 48),
        name="final",
    )(x1, peer_out, mods, final_g.reshape(1, D_MODEL))


PAST_LEN = 2048
MIXIN_TOKENS = 512
TOKEN_TILE = 256
PEER_TOKENS = 512
PEER_CHUNK = 512
TOPK_TOKENS = 256


def _channel_tail(x, conv_raw, attn, mods, group_rows, weights):
    x1, h2t, qp = _finish(x, conv_raw, attn, mods, group_rows, weights["ln_g"], weights["ln_b"],
                          weights["n2_g"], weights["wout"], weights["wq"], TOKEN_TILE)
    s1, s2, e1, e2, thr = _topk(qp, weights["keys"], TOPK_TOKENS)
    peer_out = _peer(h2t, weights["u"], weights["vt"], s1, s2, e1, e2, thr, PEER_TOKENS, PEER_CHUNK)
    return _final(x1, peer_out, mods, group_rows, weights["final_g"], TOKEN_TILE)


def kernel(x_prompt, x_sample, cache_k, cache_v, state_conv, c_prompt, c_sample, w_ada, b_ada, norm1_g, w_in,
           conv_w, conv_b, conv_ln_g, conv_ln_b, attn_out_g, w_out, norm2_g, peer_wq, peer_keys, peer_u, peer_v,
           final_g):
    b, seq, d = x_prompt.shape
    db, n_new, _ = x_sample.shape
    depth = w_ada.shape[0]
    assert depth == 1 and d == D_MODEL
    cache_len = cache_k.shape[2]
    n_sample = db * n_new
    assert seq % MIXIN_TOKENS == 0 and n_sample % PEER_TOKENS == 0

    c_rows = jnp.concatenate([c_prompt, jnp.repeat(c_sample, n_new, axis=0)], axis=0)
    mods = _ada(c_rows, w_ada[0], b_ada[0])
    mods_p = mods[:b].reshape(b, 1, N_ADA * D_MODEL)
    mods_s = mods[b:]

    weights = {
        "ln_g": conv_ln_g[0], "ln_b": conv_ln_b[0], "n2_g": norm2_g[0],
        "wout": w_out[0].astype(BF16), "wq": peer_wq[0].astype(BF16),
        "keys": peer_keys[0].astype(BF16), "u": peer_u[0].astype(BF16),
        "vt": peer_v[0].T.astype(BF16), "final_g": final_g,
    }
    w_in_bf = w_in[0].astype(BF16)

    xp = x_prompt.reshape(b * seq, d)
    tables_p = _rope_tables(np.arange(seq))
    glu, q, k, v = _mixin(xp, mods_p, seq, norm1_g[0], w_in_bf, tables_p, MIXIN_TOKENS)
    glu3 = glu.reshape(b, seq, CONV_CH)
    conv_raw = _conv_prompt(glu3, jnp.zeros((b, CONV_PAD, CONV_CH), F32), conv_w[0], conv_b[0])
    attn = _attn_prompt(q.reshape(b, seq, ATT_W), k.reshape(b, seq, ATT_W), v.reshape(b, seq, ATT_W),
                        attn_out_g[0])
    y_prompt = _channel_tail(xp, conv_raw.reshape(b * seq, CONV_CH), attn.reshape(b * seq, ATT_W),
                             mods_p, seq, weights).reshape(b, seq, d)
    keep = min(WINDOWS[2], seq)
    new_k_prompt = k.reshape(b, seq, N_HEADS, HEAD_DIM)[:, seq - keep:][None]
    new_v_prompt = v.reshape(b, seq, N_HEADS, HEAD_DIM)[:, seq - keep:][None]
    new_conv_prompt = glu3[:, seq - CONV_HIST:][None]

    xs = x_sample.reshape(n_sample, d)
    tables_s = tuple(jnp.tile(t, (TOKEN_TILE // n_new, 1)) for t in _rope_tables(PAST_LEN + np.arange(n_new)))
    glu_s, q_s, k_s, v_s = _mixin(xs, mods_s, None, norm1_g[0], w_in_bf, tables_s, TOKEN_TILE)
    glu_s3 = glu_s.reshape(db, n_new, CONV_CH)
    conv_raw_s = _conv_sample(state_conv[0], glu_s3, conv_w[0], conv_b[0])
    pad8 = lambda a: jnp.pad(a.reshape(db, n_new, ATT_W), ((0, 0), (0, S_PAD - n_new), (0, 0)))
    attn_s = _attn_sample(pad8(q_s), pad8(k_s), pad8(v_s), cache_k[0], cache_v[0], attn_out_g[0], n_new,
                          cache_len)
    attn_s = attn_s[:, :n_new].reshape(n_sample, ATT_W)
    y_sample = _channel_tail(xs, conv_raw_s.reshape(n_sample, CONV_CH), attn_s,
                             mods_s, None, weights).reshape(db, n_new, d)
    new_k_sample = k_s.reshape(db, n_new, N_HEADS, HEAD_DIM)[None]
    new_v_sample = v_s.reshape(db, n_new, N_HEADS, HEAD_DIM)[None]
    new_conv_sample = jnp.concatenate([state_conv[0], glu_s3], axis=1)[:, n_new:][None]

    return (y_prompt, y_sample, new_k_prompt, new_v_prompt, new_conv_prompt,
            new_k_sample, new_v_sample, new_conv_sample)
```

```python
import functools

import numpy as np
import jax
import jax.numpy as jnp
from jax import lax
from jax.experimental import pallas as pl
from jax.experimental.pallas import tpu as pltpu

F32 = jnp.float32
BF16 = jnp.bfloat16

D_MODEL = 2048
CONV_CH = 1024
CONV_W = 31
N_HEADS = 8
HEAD_DIM = 128
ATT_W = N_HEADS * HEAD_DIM
ROT_DIM = HEAD_DIM // 4
ROPE_THETA = 500000.0
WINDOWS = (128, 512, 2048)
DILATIONS = (1, 4, 16)
Q_BLOCK = 128
ATT_SCALE = HEAD_DIM ** -0.5
N_ADA = 6
PEER_HEADS = 8
PEER_NKEYS = 128
PEER_DK = 128
PEER_TOPK = 16
EPS = 1e-6

LANES = 128
SUBLANES = 8
MIB = 1024 * 1024

NEG_INF = float("-inf")


def _nt_dot(a, b):
    return lax.dot_general(a, b, (((1,), (1,)), ((), ())), preferred_element_type=F32)


def _rows(ref):
    return ref[0] if len(ref.shape) == 3 else ref[...]


def _params(sem, vmem_mib):
    return pltpu.CompilerParams(dimension_semantics=sem, vmem_limit_bytes=vmem_mib * MIB)


def _ada_kernel(c_ref, w_ref, b_ref, o_ref):
    c = c_ref[...]
    a = (c * jax.nn.sigmoid(c)).astype(BF16)
    o_ref[...] = jnp.dot(a, w_ref[...].astype(BF16), preferred_element_type=F32) + b_ref[...]


def _ada(c_rows, w_ada, b_ada):
    rows = c_rows.shape[0]
    n = w_ada.shape[1]
    tn = 1024
    return pl.pallas_call(
        _ada_kernel,
        grid=(n // tn,),
        in_specs=[pl.BlockSpec((rows, D_MODEL), lambda j: (0, 0)),
                  pl.BlockSpec((D_MODEL, tn), lambda j: (0, j)),
                  pl.BlockSpec((1, tn), lambda j: (0, j))],
        out_specs=pl.BlockSpec((rows, tn), lambda j: (0, j)),
        out_shape=jax.ShapeDtypeStruct((rows, n), F32),
        compiler_params=_params(("arbitrary",), 48),
        name="ada",
    )(c_rows, w_ada, b_ada.reshape(1, n))


def _rope_heads(z, cos, sin_lo, sin_hi):
    outs = []
    for h in range(N_HEADS):
        zh = z[:, h * HEAD_DIM:(h + 1) * HEAD_DIM]
        up = pltpu.roll(zh, HEAD_DIM - ROT_DIM // 2, 1)
        dn = pltpu.roll(zh, ROT_DIM // 2, 1)
        outs.append(zh * cos + up * sin_lo + dn * sin_hi)
    return jnp.concatenate(outs, axis=1)


def _mixin_kernel(x_ref, sh_ref, sc_ref, g_ref, w_ref, cos_ref, slo_ref, shi_ref,
                  glu_ref, q_ref, k_ref, v_ref, h_scr, a_scr):
    j = pl.program_id(1)

    @pl.when(j == 0)
    def _():
        x = x_ref[...]
        y = x * lax.rsqrt(jnp.mean(x * x, axis=-1, keepdims=True) + EPS)
        h = (y * g_ref[...]) * (1.0 + _rows(sc_ref)) + _rows(sh_ref)
        h_scr[...] = h.astype(BF16)

    z = jnp.dot(h_scr[...], w_ref[...], preferred_element_type=F32)

    @pl.when(j == 0)
    def _():
        a_scr[...] = z

    @pl.when(j == 1)
    def _():
        glu_ref[...] = a_scr[...] * jax.nn.sigmoid(z)

    @pl.when(j == 2)
    def _():
        q_ref[...] = _rope_heads(z, cos_ref[...], slo_ref[...], shi_ref[...])

    @pl.when(j == 3)
    def _():
        k_ref[...] = _rope_heads(z, cos_ref[...], slo_ref[...], shi_ref[...])

    @pl.when(j == 4)
    def _():
        v_ref[...] = z


def _rope_tables(pos):
    inv = ROPE_THETA ** (-np.arange(0, ROT_DIM, 2, dtype=np.float32) / ROT_DIM)
    ang = (pos.astype(np.float32)[:, None] * inv[None, :]).astype(np.float32)
    cos, sin = np.cos(ang), np.sin(ang)
    half = ROT_DIM // 2
    n = pos.shape[0]
    c = np.ones((n, HEAD_DIM), np.float32)
    c[:, :half] = cos
    c[:, half:ROT_DIM] = cos
    s_lo = np.zeros((n, HEAD_DIM), np.float32)
    s_lo[:, :half] = -sin
    s_hi = np.zeros((n, HEAD_DIM), np.float32)
    s_hi[:, half:ROT_DIM] = sin
    return jnp.asarray(c), jnp.asarray(s_lo), jnp.asarray(s_hi)


def _mod_spec(mods, col, tm, group_rows, ngrid):
    if mods.ndim == 3:
        assert group_rows % tm == 0
        per_group = group_rows // tm
        if ngrid == 1:
            return pl.BlockSpec((1, 1, D_MODEL), lambda i: (i // per_group, 0, col))
        return pl.BlockSpec((1, 1, D_MODEL), lambda i, j: (i // per_group, 0, col))
    if ngrid == 1:
        return pl.BlockSpec((tm, D_MODEL), lambda i: (i, col))
    return pl.BlockSpec((tm, D_MODEL), lambda i, j: (i, col))


def _mixin(x, mods, group_rows, norm_g, w_in_bf, tables, tm):
    t = x.shape[0]
    cos, s_lo, s_hi = tables
    pos_blocks = cos.shape[0] // tm
    tab_spec = pl.BlockSpec((tm, HEAD_DIM), lambda i, j: (i % pos_blocks, 0))
    out_spec = pl.BlockSpec((tm, ATT_W), lambda i, j: (i, 0))
    out_sds = jax.ShapeDtypeStruct((t, ATT_W), F32)
    return pl.pallas_call(
        _mixin_kernel,
        grid=(t // tm, 5),
        in_specs=[pl.BlockSpec((tm, D_MODEL), lambda i, j: (i, 0)),
                  _mod_spec(mods, 0, tm, group_rows, 2),
                  _mod_spec(mods, 1, tm, group_rows, 2),
                  pl.BlockSpec((1, D_MODEL), lambda i, j: (0, 0)),
                  pl.BlockSpec((D_MODEL, ATT_W), lambda i, j: (0, j)),
                  tab_spec, tab_spec, tab_spec],
        out_specs=[out_spec, out_spec, out_spec, out_spec],
        out_shape=[out_sds, out_sds, out_sds, out_sds],
        scratch_shapes=[pltpu.VMEM((tm, D_MODEL), BF16), pltpu.VMEM((tm, ATT_W), F32)],
        compiler_params=_params(("arbitrary", "arbitrary"), 56),
        name="mixin",
    )(x, mods, mods, norm_g.reshape(1, D_MODEL), w_in_bf, cos, s_lo, s_hi)


CONV_HIST = CONV_W - 1
CONV_PAD = 32
CONV_ROWS = 64


def _conv_prompt_kernel(g_ref, hist_ref, w_ref, b_ref, o_ref, xp_scr):
    seq = g_ref.shape[1]
    xp_scr[0:CONV_PAD, :] = hist_ref[0]
    xp_scr[CONV_PAD:CONV_PAD + seq, :] = g_ref[0]
    lead = CONV_PAD - CONV_HIST

    def body(c, carry):
        t0 = pl.multiple_of(c * CONV_ROWS, CONV_ROWS)
        win = xp_scr[pl.ds(t0, CONV_ROWS + CONV_PAD), :]
        acc = jnp.zeros((CONV_ROWS, LANES), F32) + b_ref[...]
        for sub in range(SUBLANES):
            shifted = win if sub == 0 else pltpu.roll(win, CONV_ROWS + CONV_PAD - sub, 0)
            for a in range(CONV_PAD // SUBLANES + 1):
                off = a * SUBLANES + sub
                j = off - lead
                if 0 <= j < CONV_W:
                    acc = acc + shifted[a * SUBLANES:a * SUBLANES + CONV_ROWS, :] * w_ref[j:j + 1, :]
        o_ref[0, pl.ds(t0, CONV_ROWS), :] = acc
        return carry

    lax.fori_loop(0, seq // CONV_ROWS, body, 0)


def _conv_prompt(glu3, hist_pad, conv_w, conv_b):
    b, seq, ch = glu3.shape
    return pl.pallas_call(
        _conv_prompt_kernel,
        grid=(b, ch // LANES),
        in_specs=[pl.BlockSpec((1, seq, LANES), lambda i, c: (i, 0, c)),
                  pl.BlockSpec((1, CONV_PAD, LANES), lambda i, c: (i, 0, c)),
                  pl.BlockSpec((CONV_W, LANES), lambda i, c: (0, c)),
                  pl.BlockSpec((1, LANES), lambda i, c: (0, c))],
        out_specs=pl.BlockSpec((1, seq, LANES), lambda i, c: (i, 0, c)),
        out_shape=jax.ShapeDtypeStruct((b, seq, ch), F32),
        scratch_shapes=[pltpu.VMEM((seq + CONV_PAD, LANES), F32)],
        compiler_params=_params(("arbitrary", "arbitrary"), 32),
        name="conv_prompt",
    )(glu3, hist_pad, conv_w, conv_b.reshape(1, ch))


def _conv_sample_kernel(hist_ref, g_ref, wh_ref, wn_ref, b_ref, o_ref):
    hist = hist_ref[...]
    g = g_ref[...]
    n_new = g.shape[1]
    for t in range(n_new):
        y = jnp.sum(hist * wh_ref[t][None], axis=1) + jnp.sum(g * wn_ref[t][None], axis=1)
        o_ref[t] = y + b_ref[...]


def _conv_sample(state, glu3, conv_w, conv_b):
    b, n_new, ch = glu3.shape
    wh = jnp.stack([jnp.pad(conv_w[:CONV_HIST - t], ((t, 0), (0, 0))) for t in range(n_new)])
    wn = jnp.stack([jnp.pad(conv_w[CONV_HIST - t:], ((0, n_new - 1 - t), (0, 0))) for t in range(n_new)])
    nb = 32
    return pl.pallas_call(
        _conv_sample_kernel,
        grid=(b // nb, ch // LANES),
        in_specs=[pl.BlockSpec((nb, CONV_HIST, LANES), lambda i, c: (i, 0, c)),
                  pl.BlockSpec((nb, n_new, LANES), lambda i, c: (i, 0, c)),
                  pl.BlockSpec((n_new, CONV_HIST, LANES), lambda i, c: (0, 0, c)),
                  pl.BlockSpec((n_new, n_new, LANES), lambda i, c: (0, 0, c)),
                  pl.BlockSpec((1, LANES), lambda i, c: (0, c))],
        out_specs=pl.BlockSpec((n_new, nb, LANES), lambda i, c: (0, i, c)),
        out_shape=jax.ShapeDtypeStruct((n_new, b, ch), F32),
        compiler_params=_params(("arbitrary", "arbitrary"), 32),
        name="conv_sample",
    )(state, glu3, wh, wn, conv_b.reshape(1, ch)).transpose(1, 0, 2)


ATTN_UNROLL = 4
MERGE_ROWS = 256


def _head_norm(o, g):
    return o * lax.rsqrt(jnp.mean(o * o, axis=-1, keepdims=True) + EPS) * g


def _attn_prompt_kernel(q_ref, k_ref, v_ref, g_ref, o_ref, acc_scr, m_scr, l_scr):
    seq = q_ref.shape[1]
    qi = lax.broadcasted_iota(jnp.int32, (Q_BLOCK, Q_BLOCK), 0)
    kj = lax.broadcasted_iota(jnp.int32, (Q_BLOCK, Q_BLOCK), 1)
    cur_ok = kj <= qi

    for pat, (win, dil) in enumerate(zip(WINDOWS, DILATIONS)):
        assert win // dil == Q_BLOCK
        n_blk = seq // dil // Q_BLOCK

        def rows(ref, start, dil=dil):
            if dil == 1:
                return ref[0, pl.ds(start, Q_BLOCK), :]
            return ref[0, pl.ds(start, Q_BLOCK, stride=dil), :]

        def put(ref, start, val, dil=dil, pat=pat):
            if dil == 1:
                ref[pat, pl.ds(start, Q_BLOCK), :] = val
            else:
                ref[pat, pl.ds(start, Q_BLOCK, stride=dil), :] = val

        def body(i, carry, dil=dil, n_blk=n_blk, rows=rows, put=put):
            r = i // n_blk
            n = i % n_blk
            cur = r + dil * Q_BLOCK * n
            prev = r + dil * Q_BLOCK * jnp.maximum(n - 1, 0)
            if dil == 1:
                cur = pl.multiple_of(cur, Q_BLOCK)
                prev = pl.multiple_of(prev, Q_BLOCK)
            qb = rows(q_ref, cur).astype(BF16)
            kc = rows(k_ref, cur).astype(BF16)
            vc = rows(v_ref, cur).astype(BF16)
            kp = rows(k_ref, prev).astype(BF16)
            vp = rows(v_ref, prev).astype(BF16)
            s_c = jnp.where(cur_ok, _nt_dot(qb, kc) * ATT_SCALE, NEG_INF)
            no_prev = jnp.where(n > 0, 0, Q_BLOCK)
            s_p = jnp.where(kj >= qi + no_prev, _nt_dot(qb, kp) * ATT_SCALE, NEG_INF)
            m = jnp.maximum(jnp.max(s_c, axis=-1, keepdims=True), jnp.max(s_p, axis=-1, keepdims=True))
            p_c = jnp.exp(s_c - m)
            p_p = jnp.exp(s_p - m)
            l = jnp.sum(p_c, axis=-1, keepdims=True) + jnp.sum(p_p, axis=-1, keepdims=True)
            acc = (jnp.dot(p_c.astype(BF16), vc, preferred_element_type=F32)
                   + jnp.dot(p_p.astype(BF16), vp, preferred_element_type=F32))
            put(acc_scr, cur, acc)
            put(m_scr, cur, jnp.broadcast_to(m, (Q_BLOCK, HEAD_DIM)))
            put(l_scr, cur, jnp.broadcast_to(l, (Q_BLOCK, HEAD_DIM)))
            return carry

        lax.fori_loop(0, dil * n_blk, body, 0, unroll=ATTN_UNROLL)

    n_pat = len(WINDOWS)

    def merge(c, carry):
        sl = pl.ds(pl.multiple_of(c * MERGE_ROWS, MERGE_ROWS), MERGE_ROWS)
        ms = [m_scr[p, sl, :] for p in range(n_pat)]
        top = functools.reduce(jnp.maximum, ms)
        ws = [jnp.exp(m - top) for m in ms]
        num = functools.reduce(jnp.add, [w * acc_scr[p, sl, :] for p, w in enumerate(ws)])
        den = functools.reduce(jnp.add, [w * l_scr[p, sl, :] for p, w in enumerate(ws)])
        o_ref[0, sl, :] = _head_norm(num / den, g_ref[0])
        return carry

    lax.fori_loop(0, seq // MERGE_ROWS, merge, 0)


def _attn_prompt(q3, k3, v3, head_g):
    b, seq, _ = q3.shape
    spec = pl.BlockSpec((1, seq, HEAD_DIM), lambda i, h: (i, 0, h))
    return pl.pallas_call(
        _attn_prompt_kernel,
        grid=(b, N_HEADS),
        in_specs=[spec, spec, spec, pl.BlockSpec((1, 1, HEAD_DIM), lambda i, h: (h, 0, 0))],
        out_specs=spec,
        out_shape=jax.ShapeDtypeStruct((b, seq, ATT_W), F32),
        scratch_shapes=[pltpu.VMEM((len(WINDOWS), seq, HEAD_DIM), F32)] * 3,
        compiler_params=_params(("arbitrary", "arbitrary"), 32),
        name="attn_prompt",
    )(q3, k3, v3, head_g.reshape(N_HEADS, 1, HEAD_DIM))


S_PAD = 8
TAIL = 512
N_COLS = LANES


def _attn_sample_kernel(q_ref, kn_ref, vn_ref, kt_ref, vt_ref, kd_ref, vd_ref, g_ref, o_ref,
                        k_scr, v_scr, kd2, vd2, *, n_new, past):
    n_dil = past // DILATIONS[2]
    kd2[...] = kd_ref[0].reshape(n_dil * n_new * N_HEADS, HEAD_DIM)
    vd2[...] = vd_ref[0].reshape(n_dil * n_new * N_HEADS, HEAD_DIM)
    for h in range(N_HEADS):
        lanes = slice(h * HEAD_DIM, (h + 1) * HEAD_DIM)
        k_scr[0:TAIL, lanes] = kt_ref[0, pl.ds(h, TAIL, stride=N_HEADS), :].astype(BF16)
        v_scr[0:TAIL, lanes] = vt_ref[0, pl.ds(h, TAIL, stride=N_HEADS), :].astype(BF16)
        for res in range(n_new):
            dst = slice(TAIL + res * n_dil, TAIL + (res + 1) * n_dil)
            src = pl.ds(res * N_HEADS + h, n_dil, stride=n_new * N_HEADS)
            k_scr[dst, lanes] = kd2[src, :].astype(BF16)
            v_scr[dst, lanes] = vd2[src, :].astype(BF16)
    dil0 = TAIL

    q8 = q_ref[0]
    qt = jnp.concatenate([q8] * N_HEADS + [jnp.zeros_like(q8)] * (N_COLS // S_PAD - N_HEADS), axis=0)
    row = lax.broadcasted_iota(jnp.int32, (N_COLS, ATT_W), 0)
    lane = lax.broadcasted_iota(jnp.int32, (N_COLS, ATT_W), 1)
    qbd = jnp.where(row // S_PAD == lane // HEAD_DIM, qt, 0.0).astype(BF16)

    def col_query(shape):
        return lax.broadcasted_iota(jnp.int32, shape, 1) % S_PAD % n_new

    def key_row(shape):
        return lax.broadcasted_iota(jnp.int32, shape, 0)

    groups = []
    s = _nt_dot(k_scr[0:TAIL, :], qbd) * ATT_SCALE
    r, c = key_row(s.shape), col_query(s.shape)
    base = past - TAIL
    mult = ((((base + r) % DILATIONS[1] == (past + c) % DILATIONS[1])
             & (base + r >= past + c - WINDOWS[1])).astype(F32)
            + (base + r >= past + c - WINDOWS[0]).astype(F32))
    groups.append((s, mult, v_scr[0:TAIL, :]))
    for res in range(n_new):
        rows = slice(dil0 + res * n_dil, dil0 + (res + 1) * n_dil)
        s = _nt_dot(k_scr[rows, :], qbd) * ATT_SCALE
        mult = (col_query(s.shape) == res).astype(F32)
        groups.append((s, mult, v_scr[rows, :]))
    s = _nt_dot(kn_ref[0].astype(BF16), qbd) * ATT_SCALE
    r, c = key_row(s.shape), col_query(s.shape)
    mult = (r <= c).astype(F32) + 2.0 * (r == c).astype(F32)
    groups.append((s, mult, vn_ref[0].astype(BF16)))

    m = None
    for s, mult, _ in groups:
        gm = jnp.max(jnp.where(mult > 0, s, NEG_INF), axis=0, keepdims=True)
        m = gm if m is None else jnp.maximum(m, gm)
    num = jnp.zeros((N_COLS, ATT_W), F32)
    den = jnp.zeros((N_COLS, LANES), F32)
    for s, mult, v in groups:
        p = mult * jnp.exp(jnp.where(mult > 0, s - m, NEG_INF))
        pt = p.T.astype(BF16)
        num = num + jnp.dot(pt, v, preferred_element_type=F32)
        den = den + jnp.dot(pt, jnp.ones((v.shape[0], LANES), BF16), preferred_element_type=F32)
    outs = []
    for h in range(N_HEADS):
        o = (num[h * S_PAD:(h + 1) * S_PAD, h * HEAD_DIM:(h + 1) * HEAD_DIM]
             / den[h * S_PAD:(h + 1) * S_PAD, :])
        outs.append(_head_norm(o, g_ref[h]))
    o_ref[0] = jnp.concatenate(outs, axis=1)


def _attn_sample(q8, k8, v8, cache_k, cache_v, head_g, n_new, past):
    b = q8.shape[0]
    lw = cache_k.shape[1]
    assert lw == past and lw % 16 == 0 and lw >= WINDOWS[2] and n_new <= 4
    dil = DILATIONS[2]
    ck_rows = cache_k.reshape(b, lw * N_HEADS, HEAD_DIM)
    cv_rows = cache_v.reshape(b, lw * N_HEADS, HEAD_DIM)
    ck_grp = cache_k.reshape(b, lw // dil, dil * N_HEADS, HEAD_DIM)
    cv_grp = cache_v.reshape(b, lw // dil, dil * N_HEADS, HEAD_DIM)
    new_spec = pl.BlockSpec((1, S_PAD, ATT_W), lambda i: (i, 0, 0))
    tail_spec = pl.BlockSpec((1, TAIL * N_HEADS, HEAD_DIM), lambda i: (i, lw // TAIL - 1, 0))
    dil_spec = pl.BlockSpec((1, lw // dil, n_new * N_HEADS, HEAD_DIM), lambda i: (i, 0, 0, 0))
    n_keys = TAIL + n_new * (lw // dil)
    return pl.pallas_call(
        functools.partial(_attn_sample_kernel, n_new=n_new, past=past),
        grid=(b,),
        in_specs=[new_spec, new_spec, new_spec, tail_spec, tail_spec, dil_spec, dil_spec,
                  pl.BlockSpec((N_HEADS, 1, HEAD_DIM), lambda i: (0, 0, 0))],
        out_specs=new_spec,
        out_shape=jax.ShapeDtypeStruct((b, S_PAD, ATT_W), F32),
        scratch_shapes=[pltpu.VMEM((n_keys, ATT_W), BF16), pltpu.VMEM((n_keys, ATT_W), BF16),
                        pltpu.VMEM(((lw // dil) * n_new * N_HEADS, HEAD_DIM), F32),
                        pltpu.VMEM(((lw // dil) * n_new * N_HEADS, HEAD_DIM), F32)],
        compiler_params=_params(("arbitrary",), 48),
        name="attn_sample",
    )(q8, k8, v8, ck_rows, cv_rows, ck_grp, cv_grp, head_g.reshape(N_HEADS, 1, HEAD_DIM))


def _finish_kernel(x_ref, conv_ref, attn_ref, g1_ref, sh2_ref, sc2_ref, lng_ref, lnb_ref, n2g_ref,
                   wout_ref, wq_ref, x1_ref, h2t_ref, qp_ref, cat_scr):
    y = conv_ref[...]
    mu = jnp.mean(y, axis=-1, keepdims=True)
    yc = y - mu
    var = jnp.mean(yc * yc, axis=-1, keepdims=True)
    z = yc * lax.rsqrt(var + EPS) * lng_ref[...] + lnb_ref[...]
    cat_scr[:, :CONV_CH] = (z * jax.nn.sigmoid(z)).astype(BF16)
    cat_scr[:, CONV_CH:] = attn_ref[...].astype(BF16)
    mix = jnp.dot(cat_scr[...], wout_ref[...], preferred_element_type=F32)
    x1 = x_ref[...] + (1.0 + _rows(g1_ref)) * mix
    x1_ref[...] = x1
    n = x1 * lax.rsqrt(jnp.mean(x1 * x1, axis=-1, keepdims=True) + EPS)
    h2 = (n * n2g_ref[...]) * (1.0 + _rows(sc2_ref)) + _rows(sh2_ref)
    h2t_ref[...] = h2.T.astype(BF16)
    qp_ref[...] = jnp.dot(h2.astype(BF16), wq_ref[...], preferred_element_type=F32).astype(BF16)


def _finish(x, conv_raw, attn, mods, group_rows, ln_g, ln_b, n2_g, wout_bf, wq_bf, tm):
    t = x.shape[0]
    row = lambda n: pl.BlockSpec((1, n), lambda i: (0, 0))
    const = lambda shape: pl.BlockSpec(shape, lambda i: (0, 0), pipeline_mode=pl.Buffered(1))
    tile = lambda n: pl.BlockSpec((tm, n), lambda i: (i, 0))
    return pl.pallas_call(
        _finish_kernel,
        grid=(t // tm,),
        in_specs=[tile(D_MODEL), tile(CONV_CH), tile(ATT_W),
                  _mod_spec(mods, 2, tm, group_rows, 1),
                  _mod_spec(mods, 3, tm, group_rows, 1),
                  _mod_spec(mods, 4, tm, group_rows, 1),
                  row(CONV_CH), row(CONV_CH), row(D_MODEL),
                  const((D_MODEL, D_MODEL)), const((D_MODEL, D_MODEL))],
        out_specs=[tile(D_MODEL), pl.BlockSpec((D_MODEL, tm), lambda i: (0, i)), tile(D_MODEL)],
        out_shape=[jax.ShapeDtypeStruct((t, D_MODEL), F32),
                   jax.ShapeDtypeStruct((D_MODEL, t), BF16),
                   jax.ShapeDtypeStruct((t, D_MODEL), BF16)],
        scratch_shapes=[pltpu.VMEM((tm, D_MODEL), BF16)],
        compiler_params=_params(("arbitrary",), 56),
        name="finish",
    )(x, conv_raw, attn, mods, mods, mods, ln_g.reshape(1, CONV_CH), ln_b.reshape(1, CONV_CH),
      n2_g.reshape(1, D_MODEL), wout_bf, wq_bf)


TOPK_RANK = PEER_TOPK + 1
TOPK_ROWS = -(-TOPK_RANK // SUBLANES) * SUBLANES


def _top_values(s, k, out_scr):
    out_scr[...] = jnp.full(out_scr.shape, NEG_INF, F32)
    for i in range(k):
        m = jnp.max(s, axis=0, keepdims=True)
        out_scr[i:i + 1, :] = m
        s = jnp.where(s == m, NEG_INF, s)


def _pair_candidates(a_scr, b_scr):
    tb = a_scr.shape[1]
    rows_all = lax.broadcasted_iota(jnp.int32, (TOPK_ROWS, tb), 0)
    rows_one = lax.broadcasted_iota(jnp.int32, (SUBLANES, tb), 0)
    cands = [a_scr[0:1, :] + b_scr[...],
             jnp.where(rows_all >= 1, b_scr[0:1, :] + a_scr[...], NEG_INF)]
    single = []
    for i in range(1, TOPK_RANK):
        j_max = TOPK_RANK // (i + 1) - 1
        if j_max >= 2:
            assert j_max < SUBLANES
            ok = (rows_one >= 1) & (rows_one <= j_max)
            cands.append(jnp.where(ok, a_scr[i:i + 1, :] + b_scr[0:SUBLANES, :], NEG_INF))
        elif j_max == 1:
            single.append(i)
    if single:
        assert single == list(range(single[0], single[-1] + 1)) and single[-1] < SUBLANES
        ok = (rows_one >= single[0]) & (rows_one <= single[-1])
        cands.append(jnp.where(ok, b_scr[1:2, :] + a_scr[0:SUBLANES, :], NEG_INF))
    return cands


def _column_reduce(arrays, combine, reduce_rows):
    by_rows = {}
    for a in arrays:
        by_rows[a.shape[0]] = a if a.shape[0] not in by_rows else combine(by_rows[a.shape[0]], a)
    return functools.reduce(combine, [reduce_rows(a, axis=0, keepdims=True) for a in by_rows.values()])


def _topk_kernel(qp_ref, keys_ref, s1_ref, s2_ref, e1_ref, e2_ref, thr_ref, a_scr, b_scr):
    tb = qp_ref.shape[0]
    for h in range(PEER_HEADS):
        sc = []
        for p in range(2):
            col = (h * 2 + p) * PEER_DK
            sc.append(_nt_dot(keys_ref[h, p], qp_ref[:, col:col + PEER_DK]))
        s1, s2 = sc
        _top_values(s1, TOPK_RANK, a_scr)
        _top_values(s2, TOPK_RANK, b_scr)
        cands = _pair_candidates(a_scr, b_scr)
        top = None
        cum = jnp.zeros((1, tb), F32)
        z = jnp.zeros((1, tb), F32)
        v_in = jnp.zeros((1, tb), F32)
        v_out = jnp.zeros((1, tb), F32)
        for _ in range(TOPK_RANK):
            m = _column_reduce(cands, jnp.maximum, jnp.max)
            hits = [c == m for c in cands]
            cnt = _column_reduce([jnp.where(e, 1.0, 0.0) for e in hits], jnp.add, jnp.sum)
            cands = [jnp.where(e, NEG_INF, c) for e, c in zip(hits, cands)]
            top = m if top is None else top
            take = jnp.minimum(cnt, jnp.maximum(PEER_TOPK - cum, 0.0))
            z = z + take * jnp.exp(m - top)
            reached = cum + cnt
            v_in = jnp.where((cum < PEER_TOPK) & (reached >= PEER_TOPK), m, v_in)
            v_out = jnp.where((cum < TOPK_RANK) & (reached >= TOPK_RANK), m, v_out)
            cum = reached
        s1_ref[h] = s1
        s2_ref[h] = s2
        e1_ref[h] = jnp.exp(s1 - a_scr[0:1, :])
        e2_ref[h] = jnp.exp(s2 - b_scr[0:1, :]) / z
        thr_ref[h] = 0.5 * (v_in + v_out)


def _topk(qp, keys_bf, tb):
    t = qp.shape[0]
    big = pl.BlockSpec((PEER_HEADS, PEER_NKEYS, tb), lambda i: (0, 0, i))
    big_sds = jax.ShapeDtypeStruct((PEER_HEADS, PEER_NKEYS, t), F32)
    return pl.pallas_call(
        _topk_kernel,
        grid=(t // tb,),
        in_specs=[pl.BlockSpec((tb, D_MODEL), lambda i: (i, 0)),
                  pl.BlockSpec((PEER_HEADS, 2, PEER_NKEYS, PEER_DK), lambda i: (0, 0, 0, 0))],
        out_specs=[big, big, big, big, pl.BlockSpec((PEER_HEADS, 1, tb), lambda i: (0, 0, i))],
        out_shape=[big_sds, big_sds, big_sds, big_sds, jax.ShapeDtypeStruct((PEER_HEADS, 1, t), F32)],
        scratch_shapes=[pltpu.VMEM((TOPK_ROWS, tb), F32), pltpu.VMEM((TOPK_ROWS, tb), F32)],
        compiler_params=_params(("arbitrary",), 32),
        name="peer_topk",
    )(qp, keys_bf)


_SQRT_HALF = float(np.sqrt(0.5))


def _gelu(x):
    return 0.5 * x * (1.0 + lax.erf(x * _SQRT_HALF))


MXU_TILE = 256
HID_ROWS = 512


def _peer_kernel(ht_ref, u_ref, vt_ref, s1_ref, s2_ref, e1_ref, e2_ref, thr_ref, o_ref, acc_scr, *w_refs):
    j = pl.program_id(1)
    tb = ht_ref.shape[1]
    ec = u_ref.shape[0]
    assert ec == SUBLANES * PEER_NKEYS
    n_k, n_n = ec // MXU_TILE, tb // MXU_TILE
    assert len(w_refs) == n_k * n_n

    @pl.when(j == 0)
    def _():
        acc_scr[...] = jnp.zeros_like(acc_scr)

    row0 = pl.multiple_of(j * SUBLANES, SUBLANES)
    hids = [jnp.dot(u_ref[q * HID_ROWS:(q + 1) * HID_ROWS, :], ht_ref[...], preferred_element_type=F32)
            for q in range(ec // HID_ROWS)]
    per_tile = MXU_TILE // PEER_NKEYS
    for k in range(n_k):
        for n in range(n_n):
            w_ref = w_refs[k * n_n + n]
            for a in range(per_tile):
                ii = k * per_tile + a
                q, r = divmod(ii * PEER_NKEYS, HID_ROWS)
                for b in range(MXU_TILE // LANES):
                    lc = n * (MXU_TILE // LANES) + b
                    sl = slice(lc * LANES, (lc + 1) * LANES)
                    gate = jnp.zeros((PEER_NKEYS, LANES), F32)
                    for h in range(PEER_HEADS):
                        s1_rows = s1_ref[h, pl.ds(row0, SUBLANES), sl]
                        e1_rows = e1_ref[h, pl.ds(row0, SUBLANES), sl]
                        bound = thr_ref[h, :, sl] - s1_rows[ii:ii + 1, :]
                        val = e2_ref[h, :, sl] * e1_rows[ii:ii + 1, :]
                        gate = gate + jnp.where(s2_ref[h, :, sl] >= bound, val, 0.0)
                    hid = hids[q][r:r + PEER_NKEYS, sl]
                    w_ref[a * PEER_NKEYS:(a + 1) * PEER_NKEYS, b * LANES:(b + 1) * LANES] = (
                        gate * _gelu(hid)).astype(BF16)
    w_all = jnp.concatenate([jnp.concatenate([w_refs[k * n_n + n][...] for n in range(n_n)], axis=1)
                             for k in range(n_k)], axis=0)
    acc_scr[...] += jnp.dot(vt_ref[...], w_all, preferred_element_type=F32)

    @pl.when(j == pl.num_programs(1) - 1)
    def _():
        o_ref[...] = acc_scr[...].T


def _peer(h2t, u_bf, vt_bf, s1, s2, e1, e2, thr, tb, ec):
    t = h2t.shape[1]
    n_chunks = u_bf.shape[0] // ec
    big = pl.BlockSpec((PEER_HEADS, PEER_NKEYS, tb), lambda i, j: (0, 0, i))
    n_tiles = (ec // MXU_TILE) * (tb // MXU_TILE)
    return pl.pallas_call(
        _peer_kernel,
        grid=(t // tb, n_chunks),
        in_specs=[pl.BlockSpec((D_MODEL, tb), lambda i, j: (0, i)),
                  pl.BlockSpec((ec, D_MODEL), lambda i, j: (j, 0)),
                  pl.BlockSpec((D_MODEL, ec), lambda i, j: (0, j)),
                  big, big, big, big,
                  pl.BlockSpec((PEER_HEADS, 1, tb), lambda i, j: (0, 0, i))],
        out_specs=pl.BlockSpec((tb, D_MODEL), lambda i, j: (i, 0)),
        out_shape=jax.ShapeDtypeStruct((t, D_MODEL), F32),
        scratch_shapes=[pltpu.VMEM((D_MODEL, tb), F32)] + [pltpu.VMEM((MXU_TILE, MXU_TILE), BF16)] * n_tiles,
        compiler_params=_params(("arbitrary", "arbitrary"), 56),
        name="peer",
    )(h2t, u_bf, vt_bf, s1, s2, e1, e2, thr)


def _final_kernel(x1_ref, p_ref, g2_ref, fg_ref, y_ref):
    x2 = x1_ref[...] + (1.0 + _rows(g2_ref)) * p_ref[...]
    y_ref[...] = x2 * lax.rsqrt(jnp.mean(x2 * x2, axis=-1, keepdims=True) + EPS) * fg_ref[...]


def _final(x1, peer_out, mods, group_rows, final_g, tm):
    t = x1.shape[0]
    tile = pl.BlockSpec((tm, D_MODEL), lambda i: (i, 0))
    return pl.pallas_call(
        _final_kernel,
        grid=(t // tm,),
        in_specs=[tile, tile, _mod_spec(mods, 5, tm, group_rows, 1),
                  pl.BlockSpec((1, D_MODEL), lambda i: (0, 0))],
        out_specs=tile,
        out_shape=jax.ShapeDtypeStruct((t, D_MODEL), F32),
        compiler_params=_params(("arbitrary",), 48),
        name="final",
    )(x1, peer_out, mods, final_g.reshape(1, D_MODEL))


PAST_LEN = 2048
MIXIN_TOKENS = 512
TOKEN_TILE = 256
PEER_TOKENS = 512
PEER_CHUNK = 1024
TOPK_TOKENS = 256


def _channel_tail(x, conv_raw, attn, mods, group_rows, weights):
    x1, h2t, qp = _finish(x, conv_raw, attn, mods, group_rows, weights["ln_g"], weights["ln_b"],
                          weights["n2_g"], weights["wout"], weights["wq"], TOKEN_TILE)
    s1, s2, e1, e2, thr = _topk(qp, weights["keys"], TOPK_TOKENS)
    peer_out = _peer(h2t, weights["u"], weights["vt"], s1, s2, e1, e2, thr, PEER_TOKENS, PEER_CHUNK)
    return _final(x1, peer_out, mods, group_rows, weights["final_g"], TOKEN_TILE)


def kernel(x_prompt, x_sample, cache_k, cache_v, state_conv, c_prompt, c_sample, w_ada, b_ada, norm1_g, w_in,
           conv_w, conv_b, conv_ln_g, conv_ln_b, attn_out_g, w_out, norm2_g, peer_wq, peer_keys, peer_u, peer_v,
           final_g):
    b, seq, d = x_prompt.shape
    db, n_new, _ = x_sample.shape
    depth = w_ada.shape[0]
    assert depth == 1 and d == D_MODEL
    cache_len = cache_k.shape[2]
    n_sample = db * n_new
    assert seq % MIXIN_TOKENS == 0 and n_sample % PEER_TOKENS == 0

    c_rows = jnp.concatenate([c_prompt, jnp.repeat(c_sample, n_new, axis=0)], axis=0)
    mods = _ada(c_rows, w_ada[0], b_ada[0])
    mods_p = mods[:b].reshape(b, 1, N_ADA * D_MODEL)
    mods_s = mods[b:]

    weights = {
        "ln_g": conv_ln_g[0], "ln_b": conv_ln_b[0], "n2_g": norm2_g[0],
        "wout": w_out[0].astype(BF16), "wq": peer_wq[0].astype(BF16),
        "keys": peer_keys[0].astype(BF16), "u": peer_u[0].astype(BF16),
        "vt": peer_v[0].T.astype(BF16), "final_g": final_g,
    }
    w_in_bf = w_in[0].astype(BF16)

    xp = x_prompt.reshape(b * seq, d)
    tables_p = _rope_tables(np.arange(seq))
    glu, q, k, v = _mixin(xp, mods_p, seq, norm1_g[0], w_in_bf, tables_p, MIXIN_TOKENS)
    glu3 = glu.reshape(b, seq, CONV_CH)
    conv_raw = _conv_prompt(glu3, jnp.zeros((b, CONV_PAD, CONV_CH), F32), conv_w[0], conv_b[0])
    attn = _attn_prompt(q.reshape(b, seq, ATT_W), k.reshape(b, seq, ATT_W), v.reshape(b, seq, ATT_W),
                        attn_out_g[0])
    y_prompt = _channel_tail(xp, conv_raw.reshape(b * seq, CONV_CH), attn.reshape(b * seq, ATT_W),
                             mods_p, seq, weights).reshape(b, seq, d)
    keep = min(WINDOWS[2], seq)
    new_k_prompt = k.reshape(b, seq, N_HEADS, HEAD_DIM)[:, seq - keep:][None]
    new_v_prompt = v.reshape(b, seq, N_HEADS, HEAD_DIM)[:, seq - keep:][None]
    new_conv_prompt = glu3[:, seq - CONV_HIST:][None]

    xs = x_sample.reshape(n_sample, d)
    tables_s = tuple(jnp.tile(t, (TOKEN_TILE // n_new, 1)) for t in _rope_tables(PAST_LEN + np.arange(n_new)))
    glu_s, q_s, k_s, v_s = _mixin(xs, mods_s, None, norm1_g[0], w_in_bf, tables_s, TOKEN_TILE)
    glu_s3 = glu_s.reshape(db, n_new, CONV_CH)
    conv_raw_s = _conv_sample(state_conv[0], glu_s3, conv_w[0], conv_b[0])
    pad8 = lambda a: jnp.pad(a.reshape(db, n_new, ATT_W), ((0, 0), (0, S_PAD - n_new), (0, 0)))
    attn_s = _attn_sample(pad8(q_s), pad8(k_s), pad8(v_s), cache_k[0], cache_v[0], attn_out_g[0], n_new,
                          cache_len)
    attn_s = attn_s[:, :n_new].reshape(n_sample, ATT_W)
    y_sample = _channel_tail(xs, conv_raw_s.reshape(n_sample, CONV_CH), attn_s,
                             mods_s, None, weights).reshape(db, n_new, d)
    new_k_sample = k_s.reshape(db, n_new, N_HEADS, HEAD_DIM)[None]
    new_v_sample = v_s.reshape(db, n_new, N_HEADS, HEAD_DIM)[None]
    new_conv_sample = jnp.concatenate([state_conv[0], glu_s3], axis=1)[:, n_new:][None]

    return (y_prompt, y_sample, new_k_prompt, new_v_prompt, new_conv_prompt,
            new_k_sample, new_v_sample, new_conv_sample)
```

```python
import functools

import numpy as np
import jax
import jax.numpy as jnp
from jax import lax
from jax.experimental import pallas as pl
from jax.experimental.pallas import tpu as pltpu

F32 = jnp.float32
BF16 = jnp.bfloat16

D_MODEL = 2048
CONV_CH = 1024
CONV_W = 31
N_HEADS = 8
HEAD_DIM = 128
ATT_W = N_HEADS * HEAD_DIM
ROT_DIM = HEAD_DIM // 4
ROPE_THETA = 500000.0
WINDOWS = (128, 512, 2048)
DILATIONS = (1, 4, 16)
Q_BLOCK = 128
ATT_SCALE = HEAD_DIM ** -0.5
N_ADA = 6
PEER_HEADS = 8
PEER_NKEYS = 128
PEER_DK = 128
PEER_TOPK = 16
EPS = 1e-6

LANES = 128
SUBLANES = 8
MIB = 1024 * 1024

NEG_INF = float("-inf")


def _nt_dot(a, b):
    return lax.dot_general(a, b, (((1,), (1,)), ((), ())), preferred_element_type=F32)


def _rows(ref):
    return ref[0] if len(ref.shape) == 3 else ref[...]


def _params(sem, vmem_mib):
    return pltpu.CompilerParams(dimension_semantics=sem, vmem_limit_bytes=vmem_mib * MIB)


def _ada_kernel(c_ref, w_ref, b_ref, o_ref):
    c = c_ref[...]
    a = (c * jax.nn.sigmoid(c)).astype(BF16)
    o_ref[...] = jnp.dot(a, w_ref[...].astype(BF16), preferred_element_type=F32) + b_ref[...]


def _ada(c_rows, w_ada, b_ada):
    rows = c_rows.shape[0]
    n = w_ada.shape[1]
    tn = 1024
    return pl.pallas_call(
        _ada_kernel,
        grid=(n // tn,),
        in_specs=[pl.BlockSpec((rows, D_MODEL), lambda j: (0, 0)),
                  pl.BlockSpec((D_MODEL, tn), lambda j: (0, j)),
                  pl.BlockSpec((1, tn), lambda j: (0, j))],
        out_specs=pl.BlockSpec((rows, tn), lambda j: (0, j)),
        out_shape=jax.ShapeDtypeStruct((rows, n), F32),
        compiler_params=_params(("arbitrary",), 48),
        name="ada",
    )(c_rows, w_ada, b_ada.reshape(1, n))


def _rope_heads(z, cos, sin_lo, sin_hi):
    outs = []
    for h in range(N_HEADS):
        zh = z[:, h * HEAD_DIM:(h + 1) * HEAD_DIM]
        up = pltpu.roll(zh, HEAD_DIM - ROT_DIM // 2, 1)
        dn = pltpu.roll(zh, ROT_DIM // 2, 1)
        outs.append(zh * cos + up * sin_lo + dn * sin_hi)
    return jnp.concatenate(outs, axis=1)


def _mixin_kernel(x_ref, sh_ref, sc_ref, g_ref, w_ref, cos_ref, slo_ref, shi_ref,
                  glu_ref, q_ref, k_ref, v_ref, h_scr, a_scr):
    j = pl.program_id(1)

    @pl.when(j == 0)
    def _():
        x = x_ref[...]
        y = x * lax.rsqrt(jnp.mean(x * x, axis=-1, keepdims=True) + EPS)
        h = (y * g_ref[...]) * (1.0 + _rows(sc_ref)) + _rows(sh_ref)
        h_scr[...] = h.astype(BF16)

    z = jnp.dot(h_scr[...], w_ref[...], preferred_element_type=F32)

    @pl.when(j == 0)
    def _():
        a_scr[...] = z

    @pl.when(j == 1)
    def _():
        glu_ref[...] = a_scr[...] * jax.nn.sigmoid(z)

    @pl.when(j == 2)
    def _():
        q_ref[...] = _rope_heads(z, cos_ref[...], slo_ref[...], shi_ref[...])

    @pl.when(j == 3)
    def _():
        k_ref[...] = _rope_heads(z, cos_ref[...], slo_ref[...], shi_ref[...])

    @pl.when(j == 4)
    def _():
        v_ref[...] = z


def _rope_tables(pos):
    inv = ROPE_THETA ** (-np.arange(0, ROT_DIM, 2, dtype=np.float32) / ROT_DIM)
    ang = (pos.astype(np.float32)[:, None] * inv[None, :]).astype(np.float32)
    cos, sin = np.cos(ang), np.sin(ang)
    half = ROT_DIM // 2
    n = pos.shape[0]
    c = np.ones((n, HEAD_DIM), np.float32)
    c[:, :half] = cos
    c[:, half:ROT_DIM] = cos
    s_lo = np.zeros((n, HEAD_DIM), np.float32)
    s_lo[:, :half] = -sin
    s_hi = np.zeros((n, HEAD_DIM), np.float32)
    s_hi[:, half:ROT_DIM] = sin
    return jnp.asarray(c), jnp.asarray(s_lo), jnp.asarray(s_hi)


def _mod_spec(mods, col, tm, group_rows, ngrid):
    if mods.ndim == 3:
        assert group_rows % tm == 0
        per_group = group_rows // tm
        if ngrid == 1:
            return pl.BlockSpec((1, 1, D_MODEL), lambda i: (i // per_group, 0, col))
        return pl.BlockSpec((1, 1, D_MODEL), lambda i, j: (i // per_group, 0, col))
    if ngrid == 1:
        return pl.BlockSpec((tm, D_MODEL), lambda i: (i, col))
    return pl.BlockSpec((tm, D_MODEL), lambda i, j: (i, col))


def _mixin(x, mods, group_rows, norm_g, w_in_bf, tables, tm):
    t = x.shape[0]
    cos, s_lo, s_hi = tables
    pos_blocks = cos.shape[0] // tm
    tab_spec = pl.BlockSpec((tm, HEAD_DIM), lambda i, j: (i % pos_blocks, 0))
    out_spec = pl.BlockSpec((tm, ATT_W), lambda i, j: (i, 0))
    out_sds = jax.ShapeDtypeStruct((t, ATT_W), F32)
    return pl.pallas_call(
        _mixin_kernel,
        grid=(t // tm, 5),
        in_specs=[pl.BlockSpec((tm, D_MODEL), lambda i, j: (i, 0)),
                  _mod_spec(mods, 0, tm, group_rows, 2),
                  _mod_spec(mods, 1, tm, group_rows, 2),
                  pl.BlockSpec((1, D_MODEL), lambda i, j: (0, 0)),
                  pl.BlockSpec((D_MODEL, ATT_W), lambda i, j: (0, j)),
                  tab_spec, tab_spec, tab_spec],
        out_specs=[out_spec, out_spec, out_spec, out_spec],
        out_shape=[out_sds, out_sds, out_sds, out_sds],
        scratch_shapes=[pltpu.VMEM((tm, D_MODEL), BF16), pltpu.VMEM((tm, ATT_W), F32)],
        compiler_params=_params(("arbitrary", "arbitrary"), 56),
        name="mixin",
    )(x, mods, mods, norm_g.reshape(1, D_MODEL), w_in_bf, cos, s_lo, s_hi)


CONV_HIST = CONV_W - 1
CONV_PAD = 32
CONV_ROWS = 64


def _conv_prompt_kernel(g_ref, hist_ref, w_ref, b_ref, o_ref, xp_scr):
    seq = g_ref.shape[1]
    xp_scr[0:CONV_PAD, :] = hist_ref[0]
    xp_scr[CONV_PAD:CONV_PAD + seq, :] = g_ref[0]
    lead = CONV_PAD - CONV_HIST

    def body(c, carry):
        t0 = pl.multiple_of(c * CONV_ROWS, CONV_ROWS)
        win = xp_scr[pl.ds(t0, CONV_ROWS + CONV_PAD), :]
        acc = jnp.zeros((CONV_ROWS, LANES), F32) + b_ref[...]
        for sub in range(SUBLANES):
            shifted = win if sub == 0 else pltpu.roll(win, CONV_ROWS + CONV_PAD - sub, 0)
            for a in range(CONV_PAD // SUBLANES + 1):
                off = a * SUBLANES + sub
                j = off - lead
                if 0 <= j < CONV_W:
                    acc = acc + shifted[a * SUBLANES:a * SUBLANES + CONV_ROWS, :] * w_ref[j:j + 1, :]
        o_ref[0, pl.ds(t0, CONV_ROWS), :] = acc
        return carry

    lax.fori_loop(0, seq // CONV_ROWS, body, 0)


def _conv_prompt(glu3, hist_pad, conv_w, conv_b):
    b, seq, ch = glu3.shape
    return pl.pallas_call(
        _conv_prompt_kernel,
        grid=(b, ch // LANES),
        in_specs=[pl.BlockSpec((1, seq, LANES), lambda i, c: (i, 0, c)),
                  pl.BlockSpec((1, CONV_PAD, LANES), lambda i, c: (i, 0, c)),
                  pl.BlockSpec((CONV_W, LANES), lambda i, c: (0, c)),
                  pl.BlockSpec((1, LANES), lambda i, c: (0, c))],
        out_specs=pl.BlockSpec((1, seq, LANES), lambda i, c: (i, 0, c)),
        out_shape=jax.ShapeDtypeStruct((b, seq, ch), F32),
        scratch_shapes=[pltpu.VMEM((seq + CONV_PAD, LANES), F32)],
        compiler_params=_params(("arbitrary", "arbitrary"), 32),
        name="conv_prompt",
    )(glu3, hist_pad, conv_w, conv_b.reshape(1, ch))


def _conv_sample_kernel(hist_ref, g_ref, wh_ref, wn_ref, b_ref, o_ref):
    hist = hist_ref[...]
    g = g_ref[...]
    n_new = g.shape[1]
    for t in range(n_new):
        y = jnp.sum(hist * wh_ref[t][None], axis=1) + jnp.sum(g * wn_ref[t][None], axis=1)
        o_ref[t] = y + b_ref[...]


def _conv_sample(state, glu3, conv_w, conv_b):
    b, n_new, ch = glu3.shape
    wh = jnp.stack([jnp.pad(conv_w[:CONV_HIST - t], ((t, 0), (0, 0))) for t in range(n_new)])
    wn = jnp.stack([jnp.pad(conv_w[CONV_HIST - t:], ((0, n_new - 1 - t), (0, 0))) for t in range(n_new)])
    nb = 32
    return pl.pallas_call(
        _conv_sample_kernel,
        grid=(b // nb, ch // LANES),
        in_specs=[pl.BlockSpec((nb, CONV_HIST, LANES), lambda i, c: (i, 0, c)),
                  pl.BlockSpec((nb, n_new, LANES), lambda i, c: (i, 0, c)),
                  pl.BlockSpec((n_new, CONV_HIST, LANES), lambda i, c: (0, 0, c)),
                  pl.BlockSpec((n_new, n_new, LANES), lambda i, c: (0, 0, c)),
                  pl.BlockSpec((1, LANES), lambda i, c: (0, c))],
        out_specs=pl.BlockSpec((n_new, nb, LANES), lambda i, c: (0, i, c)),
        out_shape=jax.ShapeDtypeStruct((n_new, b, ch), F32),
        compiler_params=_params(("arbitrary", "arbitrary"), 32),
        name="conv_sample",
    )(state, glu3, wh, wn, conv_b.reshape(1, ch)).transpose(1, 0, 2)


ATTN_UNROLL = 4
MERGE_ROWS = 256


def _head_norm(o, g):
    return o * lax.rsqrt(jnp.mean(o * o, axis=-1, keepdims=True) + EPS) * g


def _attn_prompt_kernel(q_ref, k_ref, v_ref, g_ref, o_ref, acc_scr, m_scr, l_scr):
    seq = q_ref.shape[1]
    qi = lax.broadcasted_iota(jnp.int32, (Q_BLOCK, 2 * Q_BLOCK), 0)
    kj2 = lax.broadcasted_iota(jnp.int32, (Q_BLOCK, 2 * Q_BLOCK), 1)

    for pat, (win, dil) in enumerate(zip(WINDOWS, DILATIONS)):
        assert win // dil == Q_BLOCK
        n_blk = seq // dil // Q_BLOCK

        def rows(ref, start, dil=dil):
            if dil == 1:
                return ref[0, pl.ds(start, Q_BLOCK), :]
            return ref[0, pl.ds(start, Q_BLOCK, stride=dil), :]

        def put(ref, start, val, dil=dil, pat=pat):
            if dil == 1:
                ref[pat, pl.ds(start, Q_BLOCK), :] = val
            else:
                ref[pat, pl.ds(start, Q_BLOCK, stride=dil), :] = val

        def body(i, carry, dil=dil, n_blk=n_blk, rows=rows, put=put):
            r = i // n_blk
            n = i % n_blk
            cur = r + dil * Q_BLOCK * n
            prev = r + dil * Q_BLOCK * jnp.maximum(n - 1, 0)
            if dil == 1:
                cur = pl.multiple_of(cur, Q_BLOCK)
                prev = pl.multiple_of(prev, Q_BLOCK)
            qb = rows(q_ref, cur).astype(BF16)
            k2 = jnp.concatenate([rows(k_ref, prev), rows(k_ref, cur)], axis=0).astype(BF16)
            v2 = jnp.concatenate([rows(v_ref, prev), rows(v_ref, cur)], axis=0).astype(BF16)
            has_prev = jnp.minimum(n, 1)
            lo = qi * has_prev + Q_BLOCK * (1 - has_prev)
            s = _nt_dot(qb, k2) * ATT_SCALE
            s = jnp.where(kj2 >= lo, jnp.where(kj2 <= qi + Q_BLOCK, s, NEG_INF), NEG_INF)
            m = jnp.max(s, axis=-1, keepdims=True)
            p = jnp.exp(s - m).astype(BF16)
            v_ext = jnp.concatenate([v2, jnp.ones_like(v2)], axis=1)
            acc = jnp.dot(p, v_ext, preferred_element_type=F32)
            put(acc_scr, cur, acc[:, :HEAD_DIM])
            put(m_scr, cur, jnp.broadcast_to(m, (Q_BLOCK, HEAD_DIM)))
            put(l_scr, cur, acc[:, HEAD_DIM:])
            return carry

        lax.fori_loop(0, dil * n_blk, body, 0, unroll=ATTN_UNROLL)

    n_pat = len(WINDOWS)

    def merge(c, carry):
        sl = pl.ds(pl.multiple_of(c * MERGE_ROWS, MERGE_ROWS), MERGE_ROWS)
        ms = [m_scr[p, sl, :] for p in range(n_pat)]
        top = functools.reduce(jnp.maximum, ms)
        ws = [jnp.exp(m - top) for m in ms]
        num = functools.reduce(jnp.add, [w * acc_scr[p, sl, :] for p, w in enumerate(ws)])
        den = functools.reduce(jnp.add, [w * l_scr[p, sl, :] for p, w in enumerate(ws)])
        o_ref[0, sl, :] = _head_norm(num / den, g_ref[0])
        return carry

    lax.fori_loop(0, seq // MERGE_ROWS, merge, 0)


def _attn_prompt(q3, k3, v3, head_g):
    b, seq, _ = q3.shape
    spec = pl.BlockSpec((1, seq, HEAD_DIM), lambda i, h: (i, 0, h))
    return pl.pallas_call(
        _attn_prompt_kernel,
        grid=(b, N_HEADS),
        in_specs=[spec, spec, spec, pl.BlockSpec((1, 1, HEAD_DIM), lambda i, h: (h, 0, 0))],
        out_specs=spec,
        out_shape=jax.ShapeDtypeStruct((b, seq, ATT_W), F32),
        scratch_shapes=[pltpu.VMEM((len(WINDOWS), seq, HEAD_DIM), F32)] * 3,
        compiler_params=_params(("arbitrary", "arbitrary"), 32),
        name="attn_prompt",
    )(q3, k3, v3, head_g.reshape(N_HEADS, 1, HEAD_DIM))


S_PAD = 8
TAIL = 512
N_COLS = LANES


def _attn_sample_kernel(q_ref, kn_ref, vn_ref, kt_ref, vt_ref, kd_ref, vd_ref, g_ref, o_ref,
                        k_scr, v_scr, kd2, vd2, *, n_new, past):
    n_dil = past // DILATIONS[2]
    kd2[...] = kd_ref[0].reshape(n_dil * n_new * N_HEADS, HEAD_DIM)
    vd2[...] = vd_ref[0].reshape(n_dil * n_new * N_HEADS, HEAD_DIM)
    for h in range(N_HEADS):
        lanes = slice(h * HEAD_DIM, (h + 1) * HEAD_DIM)
        k_scr[0:TAIL, lanes] = kt_ref[0, pl.ds(h, TAIL, stride=N_HEADS), :].astype(BF16)
        v_scr[0:TAIL, lanes] = vt_ref[0, pl.ds(h, TAIL, stride=N_HEADS), :].astype(BF16)
        for res in range(n_new):
            dst = slice(TAIL + res * n_dil, TAIL + (res + 1) * n_dil)
            src = pl.ds(res * N_HEADS + h, n_dil, stride=n_new * N_HEADS)
            k_scr[dst, lanes] = kd2[src, :].astype(BF16)
            v_scr[dst, lanes] = vd2[src, :].astype(BF16)
    dil0 = TAIL

    q8 = q_ref[0]
    qt = jnp.concatenate([q8] * N_HEADS + [jnp.zeros_like(q8)] * (N_COLS // S_PAD - N_HEADS), axis=0)
    row = lax.broadcasted_iota(jnp.int32, (N_COLS, ATT_W), 0)
    lane = lax.broadcasted_iota(jnp.int32, (N_COLS, ATT_W), 1)
    qbd = jnp.where(row // S_PAD == lane // HEAD_DIM, qt, 0.0).astype(BF16)

    def col_query(shape):
        return lax.broadcasted_iota(jnp.int32, shape, 1) % S_PAD % n_new

    def key_row(shape):
        return lax.broadcasted_iota(jnp.int32, shape, 0)

    groups = []
    s = _nt_dot(k_scr[0:TAIL, :], qbd) * ATT_SCALE
    r, c = key_row(s.shape), col_query(s.shape)
    base = past - TAIL
    mult = ((((base + r) % DILATIONS[1] == (past + c) % DILATIONS[1])
             & (base + r >= past + c - WINDOWS[1])).astype(F32)
            + (base + r >= past + c - WINDOWS[0]).astype(F32))
    groups.append((s, mult, v_scr[0:TAIL, :]))
    for res in range(n_new):
        rows = slice(dil0 + res * n_dil, dil0 + (res + 1) * n_dil)
        s = _nt_dot(k_scr[rows, :], qbd) * ATT_SCALE
        mult = (col_query(s.shape) == res).astype(F32)
        groups.append((s, mult, v_scr[rows, :]))
    s = _nt_dot(kn_ref[0].astype(BF16), qbd) * ATT_SCALE
    r, c = key_row(s.shape), col_query(s.shape)
    mult = (r <= c).astype(F32) + 2.0 * (r == c).astype(F32)
    groups.append((s, mult, vn_ref[0].astype(BF16)))

    m = None
    for s, mult, _ in groups:
        gm = jnp.max(jnp.where(mult > 0, s, NEG_INF), axis=0, keepdims=True)
        m = gm if m is None else jnp.maximum(m, gm)
    num = jnp.zeros((N_COLS, ATT_W), F32)
    den = jnp.zeros((N_COLS, LANES), F32)
    for s, mult, v in groups:
        p = mult * jnp.exp(jnp.where(mult > 0, s - m, NEG_INF))
        pt = p.T.astype(BF16)
        num = num + jnp.dot(pt, v, preferred_element_type=F32)
        den = den + jnp.dot(pt, jnp.ones((v.shape[0], LANES), BF16), preferred_element_type=F32)
    outs = []
    for h in range(N_HEADS):
        o = (num[h * S_PAD:(h + 1) * S_PAD, h * HEAD_DIM:(h + 1) * HEAD_DIM]
             / den[h * S_PAD:(h + 1) * S_PAD, :])
        outs.append(_head_norm(o, g_ref[h]))
    o_ref[0] = jnp.concatenate(outs, axis=1)


def _attn_sample(q8, k8, v8, cache_k, cache_v, head_g, n_new, past):
    b = q8.shape[0]
    lw = cache_k.shape[1]
    assert lw == past and lw % 16 == 0 and lw >= WINDOWS[2] and n_new <= 4
    dil = DILATIONS[2]
    ck_rows = cache_k.reshape(b, lw * N_HEADS, HEAD_DIM)
    cv_rows = cache_v.reshape(b, lw * N_HEADS, HEAD_DIM)
    ck_grp = cache_k.reshape(b, lw // dil, dil * N_HEADS, HEAD_DIM)
    cv_grp = cache_v.reshape(b, lw // dil, dil * N_HEADS, HEAD_DIM)
    new_spec = pl.BlockSpec((1, S_PAD, ATT_W), lambda i: (i, 0, 0))
    tail_spec = pl.BlockSpec((1, TAIL * N_HEADS, HEAD_DIM), lambda i: (i, lw // TAIL - 1, 0))
    dil_spec = pl.BlockSpec((1, lw // dil, n_new * N_HEADS, HEAD_DIM), lambda i: (i, 0, 0, 0))
    n_keys = TAIL + n_new * (lw // dil)
    return pl.pallas_call(
        functools.partial(_attn_sample_kernel, n_new=n_new, past=past),
        grid=(b,),
        in_specs=[new_spec, new_spec, new_spec, tail_spec, tail_spec, dil_spec, dil_spec,
                  pl.BlockSpec((N_HEADS, 1, HEAD_DIM), lambda i: (0, 0, 0))],
        out_specs=new_spec,
        out_shape=jax.ShapeDtypeStruct((b, S_PAD, ATT_W), F32),
        scratch_shapes=[pltpu.VMEM((n_keys, ATT_W), BF16), pltpu.VMEM((n_keys, ATT_W), BF16),
                        pltpu.VMEM(((lw // dil) * n_new * N_HEADS, HEAD_DIM), F32),
                        pltpu.VMEM(((lw // dil) * n_new * N_HEADS, HEAD_DIM), F32)],
        compiler_params=_params(("arbitrary",), 48),
        name="attn_sample",
    )(q8, k8, v8, ck_rows, cv_rows, ck_grp, cv_grp, head_g.reshape(N_HEADS, 1, HEAD_DIM))


def _finish_kernel(x_ref, conv_ref, attn_ref, g1_ref, sh2_ref, sc2_ref, lng_ref, lnb_ref, n2g_ref,
                   wout_ref, wq_ref, x1_ref, h2t_ref, qp_ref, cat_scr):
    y = conv_ref[...]
    mu = jnp.mean(y, axis=-1, keepdims=True)
    yc = y - mu
    var = jnp.mean(yc * yc, axis=-1, keepdims=True)
    z = yc * lax.rsqrt(var + EPS) * lng_ref[...] + lnb_ref[...]
    cat_scr[:, :CONV_CH] = (z * jax.nn.sigmoid(z)).astype(BF16)
    cat_scr[:, CONV_CH:] = attn_ref[...].astype(BF16)
    mix = jnp.dot(cat_scr[...], wout_ref[...], preferred_element_type=F32)
    x1 = x_ref[...] + (1.0 + _rows(g1_ref)) * mix
    x1_ref[...] = x1
    n = x1 * lax.rsqrt(jnp.mean(x1 * x1, axis=-1, keepdims=True) + EPS)
    h2 = (n * n2g_ref[...]) * (1.0 + _rows(sc2_ref)) + _rows(sh2_ref)
    h2t_ref[...] = h2.T.astype(BF16)
    qp_ref[...] = jnp.dot(h2.astype(BF16), wq_ref[...], preferred_element_type=F32).astype(BF16)


def _finish(x, conv_raw, attn, mods, group_rows, ln_g, ln_b, n2_g, wout_bf, wq_bf, tm):
    t = x.shape[0]
    row = lambda n: pl.BlockSpec((1, n), lambda i: (0, 0))
    const = lambda shape: pl.BlockSpec(shape, lambda i: (0, 0), pipeline_mode=pl.Buffered(1))
    tile = lambda n: pl.BlockSpec((tm, n), lambda i: (i, 0))
    return pl.pallas_call(
        _finish_kernel,
        grid=(t // tm,),
        in_specs=[tile(D_MODEL), tile(CONV_CH), tile(ATT_W),
                  _mod_spec(mods, 2, tm, group_rows, 1),
                  _mod_spec(mods, 3, tm, group_rows, 1),
                  _mod_spec(mods, 4, tm, group_rows, 1),
                  row(CONV_CH), row(CONV_CH), row(D_MODEL),
                  const((D_MODEL, D_MODEL)), const((D_MODEL, D_MODEL))],
        out_specs=[tile(D_MODEL), pl.BlockSpec((D_MODEL, tm), lambda i: (0, i)), tile(D_MODEL)],
        out_shape=[jax.ShapeDtypeStruct((t, D_MODEL), F32),
                   jax.ShapeDtypeStruct((D_MODEL, t), BF16),
                   jax.ShapeDtypeStruct((t, D_MODEL), BF16)],
        scratch_shapes=[pltpu.VMEM((tm, D_MODEL), BF16)],
        compiler_params=_params(("arbitrary",), 56),
        name="finish",
    )(x, conv_raw, attn, mods, mods, mods, ln_g.reshape(1, CONV_CH), ln_b.reshape(1, CONV_CH),
      n2_g.reshape(1, D_MODEL), wout_bf, wq_bf)


TOPK_RANK = PEER_TOPK + 1
TOPK_ROWS = -(-TOPK_RANK // SUBLANES) * SUBLANES


def _sorting_network(n):
    pairs = []

    def merge(lo, length, r):
        step = 2 * r
        if step < length:
            merge(lo, length, step)
            merge(lo + r, length, step)
            pairs.extend((i, i + r) for i in range(lo + r, lo + length - r, step))
        else:
            pairs.append((lo, lo + r))

    def sort(lo, length):
        if length > 1:
            sort(lo, length // 2)
            sort(lo + length // 2, length // 2)
            merge(lo, length, 1)

    sort(0, n)
    return pairs


def _top_values(s, k, out_scr):
    n_tiles = s.shape[0] // SUBLANES
    v = [s[r * SUBLANES:(r + 1) * SUBLANES, :] for r in range(n_tiles)]
    for i, j in _sorting_network(n_tiles):
        v[i], v[j] = jnp.maximum(v[i], v[j]), jnp.minimum(v[i], v[j])
    out_scr[...] = jnp.full(out_scr.shape, NEG_INF, F32)
    depth = min(n_tiles, k)
    v = v[:depth]
    for i in range(k):
        m = jnp.max(v[0], axis=0, keepdims=True)
        out_scr[i:i + 1, :] = m
        pop = v[0] == m
        live = min(depth, k - i)
        for r in range(live):
            below = v[r + 1] if r + 1 < depth else NEG_INF
            v[r] = jnp.where(pop, below, v[r])


def _pair_candidates(a_scr, b_scr):
    tb = a_scr.shape[1]
    rows_all = lax.broadcasted_iota(jnp.int32, (TOPK_ROWS, tb), 0)
    rows_one = lax.broadcasted_iota(jnp.int32, (SUBLANES, tb), 0)
    cands = [a_scr[0:1, :] + b_scr[...],
             jnp.where(rows_all >= 1, b_scr[0:1, :] + a_scr[...], NEG_INF)]
    single = []
    for i in range(1, TOPK_RANK):
        j_max = TOPK_RANK // (i + 1) - 1
        if j_max >= 2:
            assert j_max < SUBLANES
            ok = (rows_one >= 1) & (rows_one <= j_max)
            cands.append(jnp.where(ok, a_scr[i:i + 1, :] + b_scr[0:SUBLANES, :], NEG_INF))
        elif j_max == 1:
            single.append(i)
    if single:
        assert single == list(range(single[0], single[-1] + 1)) and single[-1] < SUBLANES
        ok = (rows_one >= single[0]) & (rows_one <= single[-1])
        cands.append(jnp.where(ok, b_scr[1:2, :] + a_scr[0:SUBLANES, :], NEG_INF))
    return cands


def _column_reduce(arrays, combine, reduce_rows):
    by_rows = {}
    for a in arrays:
        by_rows[a.shape[0]] = a if a.shape[0] not in by_rows else combine(by_rows[a.shape[0]], a)
    return functools.reduce(combine, [reduce_rows(a, axis=0, keepdims=True) for a in by_rows.values()])


def _topk_kernel(qp_ref, keys_ref, s1_ref, s2_ref, e1_ref, e2_ref, thr_ref, a_scr, b_scr):
    tb = qp_ref.shape[0]
    for h in range(PEER_HEADS):
        sc = []
        for p in range(2):
            col = (h * 2 + p) * PEER_DK
            sc.append(_nt_dot(keys_ref[h, p], qp_ref[:, col:col + PEER_DK]))
        s1, s2 = sc
        _top_values(s1, TOPK_RANK, a_scr)
        _top_values(s2, TOPK_RANK, b_scr)
        cands = _pair_candidates(a_scr, b_scr)
        top = None
        cum = jnp.zeros((1, tb), F32)
        z = jnp.zeros((1, tb), F32)
        v_in = jnp.zeros((1, tb), F32)
        v_out = jnp.zeros((1, tb), F32)
        for _ in range(TOPK_RANK):
            m = _column_reduce(cands, jnp.maximum, jnp.max)
            hits = [c == m for c in cands]
            cnt = _column_reduce([jnp.where(e, 1.0, 0.0) for e in hits], jnp.add, jnp.sum)
            cands = [jnp.where(e, NEG_INF, c) for e, c in zip(hits, cands)]
            top = m if top is None else top
            take = jnp.minimum(cnt, jnp.maximum(PEER_TOPK - cum, 0.0))
            z = z + take * jnp.exp(m - top)
            reached = cum + cnt
            v_in = jnp.where((cum < PEER_TOPK) & (reached >= PEER_TOPK), m, v_in)
            v_out = jnp.where((cum < TOPK_RANK) & (reached >= TOPK_RANK), m, v_out)
            cum = reached
        s1_ref[h] = s1
        s2_ref[h] = s2
        e1_ref[h] = jnp.exp(s1 - a_scr[0:1, :])
        e2_ref[h] = jnp.exp(s2 - b_scr[0:1, :]) / z
        thr_ref[h] = 0.5 * (v_in + v_out)


def _topk(qp, keys_bf, tb):
    t = qp.shape[0]
    big = pl.BlockSpec((PEER_HEADS, PEER_NKEYS, tb), lambda i: (0, 0, i))
    big_sds = jax.ShapeDtypeStruct((PEER_HEADS, PEER_NKEYS, t), F32)
    return pl.pallas_call(
        _topk_kernel,
        grid=(t // tb,),
        in_specs=[pl.BlockSpec((tb, D_MODEL), lambda i: (i, 0)),
                  pl.BlockSpec((PEER_HEADS, 2, PEER_NKEYS, PEER_DK), lambda i: (0, 0, 0, 0))],
        out_specs=[big, big, big, big, pl.BlockSpec((PEER_HEADS, 1, tb), lambda i: (0, 0, i))],
        out_shape=[big_sds, big_sds, big_sds, big_sds, jax.ShapeDtypeStruct((PEER_HEADS, 1, t), F32)],
        scratch_shapes=[pltpu.VMEM((TOPK_ROWS, tb), F32), pltpu.VMEM((TOPK_ROWS, tb), F32)],
        compiler_params=_params(("arbitrary",), 32),
        name="peer_topk",
    )(qp, keys_bf)


_SQRT_HALF = float(np.sqrt(0.5))


def _gelu(x):
    return 0.5 * x * (1.0 + lax.erf(x * _SQRT_HALF))


MXU_TILE = 256
HID_GROUPS = (512, 512)


def _peer_kernel(ht_ref, u_ref, vt_ref, s1_ref, s2_ref, e1_ref, e2_ref, thr_ref, o_ref, acc_scr, *w_refs):
    j = pl.program_id(1)
    tb = ht_ref.shape[1]
    ec = u_ref.shape[0]
    assert ec == SUBLANES * PEER_NKEYS
    n_k, n_n = ec // MXU_TILE, tb // MXU_TILE
    assert len(w_refs) == n_k * n_n

    @pl.when(j == 0)
    def _():
        acc_scr[...] = jnp.zeros_like(acc_scr)

    row0 = pl.multiple_of(j * SUBLANES, SUBLANES)
    assert sum(HID_GROUPS) == ec
    starts = [sum(HID_GROUPS[:g]) for g in range(len(HID_GROUPS))]
    hids = [jnp.dot(u_ref[s:s + n, :], ht_ref[...], preferred_element_type=F32)
            for s, n in zip(starts, HID_GROUPS)]
    per_tile = MXU_TILE // PEER_NKEYS
    for k in range(n_k):
        for n in range(n_n):
            w_ref = w_refs[k * n_n + n]
            for a in range(per_tile):
                ii = k * per_tile + a
                q = max(g for g, s in enumerate(starts) if s <= ii * PEER_NKEYS)
                r = ii * PEER_NKEYS - starts[q]
                for b in range(MXU_TILE // LANES):
                    lc = n * (MXU_TILE // LANES) + b
                    sl = slice(lc * LANES, (lc + 1) * LANES)
                    gate = jnp.zeros((PEER_NKEYS, LANES), F32)
                    for h in range(PEER_HEADS):
                        s1_rows = s1_ref[h, pl.ds(row0, SUBLANES), sl]
                        e1_rows = e1_ref[h, pl.ds(row0, SUBLANES), sl]
                        bound = thr_ref[h, :, sl] - s1_rows[ii:ii + 1, :]
                        val = e2_ref[h, :, sl] * e1_rows[ii:ii + 1, :]
                        gate = gate + jnp.where(s2_ref[h, :, sl] >= bound, val, 0.0)
                    hid = hids[q][r:r + PEER_NKEYS, sl]
                    w_ref[a * PEER_NKEYS:(a + 1) * PEER_NKEYS, b * LANES:(b + 1) * LANES] = (
                        gate * _gelu(hid)).astype(BF16)
    w_all = jnp.concatenate([jnp.concatenate([w_refs[k * n_n + n][...] for n in range(n_n)], axis=1)
                             for k in range(n_k)], axis=0)
    acc_scr[...] += jnp.dot(vt_ref[...], w_all, preferred_element_type=F32)

    @pl.when(j == pl.num_programs(1) - 1)
    def _():
        o_ref[...] = acc_scr[...].T


def _peer(h2t, u_bf, vt_bf, s1, s2, e1, e2, thr, tb, ec):
    t = h2t.shape[1]
    n_chunks = u_bf.shape[0] // ec
    big = pl.BlockSpec((PEER_HEADS, PEER_NKEYS, tb), lambda i, j: (0, 0, i))
    n_tiles = (ec // MXU_TILE) * (tb // MXU_TILE)
    return pl.pallas_call(
        _peer_kernel,
        grid=(t // tb, n_chunks),
        in_specs=[pl.BlockSpec((D_MODEL, tb), lambda i, j: (0, i)),
                  pl.BlockSpec((ec, D_MODEL), lambda i, j: (j, 0)),
                  pl.BlockSpec((D_MODEL, ec), lambda i, j: (0, j)),
                  big, big, big, big,
                  pl.BlockSpec((PEER_HEADS, 1, tb), lambda i, j: (0, 0, i))],
        out_specs=pl.BlockSpec((tb, D_MODEL), lambda i, j: (i, 0)),
        out_shape=jax.ShapeDtypeStruct((t, D_MODEL), F32),
        scratch_shapes=[pltpu.VMEM((D_MODEL, tb), F32)] + [pltpu.VMEM((MXU_TILE, MXU_TILE), BF16)] * n_tiles,
        compiler_params=_params(("arbitrary", "arbitrary"), 56),
        name="peer",
    )(h2t, u_bf, vt_bf, s1, s2, e1, e2, thr)


def _final_kernel(x1_ref, p_ref, g2_ref, fg_ref, y_ref):
    x2 = x1_ref[...] + (1.0 + _rows(g2_ref)) * p_ref[...]
    y_ref[...] = x2 * lax.rsqrt(jnp.mean(x2 * x2, axis=-1, keepdims=True) + EPS) * fg_ref[...]


def _final(x1, peer_out, mods, group_rows, final_g, tm):
    t = x1.shape[0]
    tile = pl.BlockSpec((tm, D_MODEL), lambda i: (i, 0))
    return pl.pallas_call(
        _final_kernel,
        grid=(t // tm,),
        in_specs=[tile, tile, _mod_spec(mods, 5, tm, group_rows, 1),
                  pl.BlockSpec((1, D_MODEL), lambda i: (0, 0))],
        out_specs=tile,
        out_shape=jax.ShapeDtypeStruct((t, D_MODEL), F32),
        compiler_params=_params(("arbitrary",), 48),
        name="final",
    )(x1, peer_out, mods, final_g.reshape(1, D_MODEL))


PAST_LEN = 2048
MIXIN_TOKENS = 512
TOKEN_TILE = 256
PEER_TOKENS = 512
PEER_CHUNK = 1024
TOPK_TOKENS = 256


def _channel_tail(x, conv_raw, attn, mods, group_rows, weights):
    x1, h2t, qp = _finish(x, conv_raw, attn, mods, group_rows, weights["ln_g"], weights["ln_b"],
                          weights["n2_g"], weights["wout"], weights["wq"], TOKEN_TILE)
    s1, s2, e1, e2, thr = _topk(qp, weights["keys"], TOPK_TOKENS)
    peer_out = _peer(h2t, weights["u"], weights["vt"], s1, s2, e1, e2, thr, PEER_TOKENS, PEER_CHUNK)
    return _final(x1, peer_out, mods, group_rows, weights["final_g"], TOKEN_TILE)


def kernel(x_prompt, x_sample, cache_k, cache_v, state_conv, c_prompt, c_sample, w_ada, b_ada, norm1_g, w_in,
           conv_w, conv_b, conv_ln_g, conv_ln_b, attn_out_g, w_out, norm2_g, peer_wq, peer_keys, peer_u, peer_v,
           final_g):
    b, seq, d = x_prompt.shape
    db, n_new, _ = x_sample.shape
    depth = w_ada.shape[0]
    assert depth == 1 and d == D_MODEL
    cache_len = cache_k.shape[2]
    n_sample = db * n_new
    assert seq % MIXIN_TOKENS == 0 and n_sample % PEER_TOKENS == 0

    c_rows = jnp.concatenate([c_prompt, jnp.repeat(c_sample, n_new, axis=0)], axis=0)
    mods = _ada(c_rows, w_ada[0], b_ada[0])
    mods_p = mods[:b].reshape(b, 1, N_ADA * D_MODEL)
    mods_s = mods[b:]

    weights = {
        "ln_g": conv_ln_g[0], "ln_b": conv_ln_b[0], "n2_g": norm2_g[0],
        "wout": w_out[0].astype(BF16), "wq": peer_wq[0].astype(BF16),
        "keys": peer_keys[0].astype(BF16), "u": peer_u[0].astype(BF16),
        "vt": peer_v[0].T.astype(BF16), "final_g": final_g,
    }
    w_in_bf = w_in[0].astype(BF16)

    xp = x_prompt.reshape(b * seq, d)
    tables_p = _rope_tables(np.arange(seq))
    glu, q, k, v = _mixin(xp, mods_p, seq, norm1_g[0], w_in_bf, tables_p, MIXIN_TOKENS)
    glu3 = glu.reshape(b, seq, CONV_CH)
    conv_raw = _conv_prompt(glu3, jnp.zeros((b, CONV_PAD, CONV_CH), F32), conv_w[0], conv_b[0])
    attn = _attn_prompt(q.reshape(b, seq, ATT_W), k.reshape(b, seq, ATT_W), v.reshape(b, seq, ATT_W),
                        attn_out_g[0])
    y_prompt = _channel_tail(xp, conv_raw.reshape(b * seq, CONV_CH), attn.reshape(b * seq, ATT_W),
                             mods_p, seq, weights).reshape(b, seq, d)
    keep = min(WINDOWS[2], seq)
    new_k_prompt = k.reshape(b, seq, N_HEADS, HEAD_DIM)[:, seq - keep:][None]
    new_v_prompt = v.reshape(b, seq, N_HEADS, HEAD_DIM)[:, seq - keep:][None]
    new_conv_prompt = glu3[:, seq - CONV_HIST:][None]

    xs = x_sample.reshape(n_sample, d)
    tables_s = tuple(jnp.tile(t, (TOKEN_TILE // n_new, 1)) for t in _rope_tables(PAST_LEN + np.arange(n_new)))
    glu_s, q_s, k_s, v_s = _mixin(xs, mods_s, None, norm1_g[0], w_in_bf, tables_s, TOKEN_TILE)
    glu_s3 = glu_s.reshape(db, n_new, CONV_CH)
    conv_raw_s = _conv_sample(state_conv[0], glu_s3, conv_w[0], conv_b[0])
    pad8 = lambda a: jnp.pad(a.reshape(db, n_new, ATT_W), ((0, 0), (0, S_PAD - n_new), (0, 0)))
    attn_s = _attn_sample(pad8(q_s), pad8(k_s), pad8(v_s), cache_k[0], cache_v[0], attn_out_g[0], n_new,
                          cache_len)
    attn_s = attn_s[:, :n_new].reshape(n_sample, ATT_W)
    y_sample = _channel_tail(xs, conv_raw_s.reshape(n_sample, CONV_CH), attn_s,
                             mods_s, None, weights).reshape(db, n_new, d)
    new_k_sample = k_s.reshape(db, n_new, N_HEADS, HEAD_DIM)[None]
    new_v_sample = v_s.reshape(db, n_new, N_HEADS, HEAD_DIM)[None]
    new_conv_sample = jnp.concatenate([state_conv[0], glu_s3], axis=1)[:, n_new:][None]

    return (y_prompt, y_sample, new_k_prompt, new_v_prompt, new_conv_prompt,
            new_k_sample, new_v_sample, new_conv_sample)
```

```python
import functools

import numpy as np
import jax
import jax.numpy as jnp
from jax import lax
from jax.experimental import pallas as pl
from jax.experimental.pallas import tpu as pltpu

F32 = jnp.float32
BF16 = jnp.bfloat16

D_MODEL = 2048
CONV_CH = 1024
CONV_W = 31
N_HEADS = 8
HEAD_DIM = 128
ATT_W = N_HEADS * HEAD_DIM
ROT_DIM = HEAD_DIM // 4
ROPE_THETA = 500000.0
WINDOWS = (128, 512, 2048)
DILATIONS = (1, 4, 16)
Q_BLOCK = 128
ATT_SCALE = HEAD_DIM ** -0.5
N_ADA = 6
PEER_HEADS = 8
PEER_NKEYS = 128
PEER_DK = 128
PEER_TOPK = 16
EPS = 1e-6

LANES = 128
SUBLANES = 8
MIB = 1024 * 1024

NEG_INF = float("-inf")


def _nt_dot(a, b):
    return lax.dot_general(a, b, (((1,), (1,)), ((), ())), preferred_element_type=F32)


def _rows(ref):
    return ref[0] if len(ref.shape) == 3 else ref[...]


def _params(sem, vmem_mib):
    return pltpu.CompilerParams(dimension_semantics=sem, vmem_limit_bytes=vmem_mib * MIB)


def _ada_kernel(c_ref, w_ref, b_ref, o_ref):
    c = c_ref[...]
    a = (c * jax.nn.sigmoid(c)).astype(BF16)
    o_ref[...] = jnp.dot(a, w_ref[...].astype(BF16), preferred_element_type=F32) + b_ref[...]


def _ada(c_rows, w_ada, b_ada):
    rows = c_rows.shape[0]
    n = w_ada.shape[1]
    tn = 1024
    return pl.pallas_call(
        _ada_kernel,
        grid=(n // tn,),
        in_specs=[pl.BlockSpec((rows, D_MODEL), lambda j: (0, 0)),
                  pl.BlockSpec((D_MODEL, tn), lambda j: (0, j)),
                  pl.BlockSpec((1, tn), lambda j: (0, j))],
        out_specs=pl.BlockSpec((rows, tn), lambda j: (0, j)),
        out_shape=jax.ShapeDtypeStruct((rows, n), F32),
        compiler_params=_params(("arbitrary",), 48),
        name="ada",
    )(c_rows, w_ada, b_ada.reshape(1, n))


def _rope_heads(z, cos, sin_lo, sin_hi):
    outs = []
    for h in range(N_HEADS):
        zh = z[:, h * HEAD_DIM:(h + 1) * HEAD_DIM]
        up = pltpu.roll(zh, HEAD_DIM - ROT_DIM // 2, 1)
        dn = pltpu.roll(zh, ROT_DIM // 2, 1)
        outs.append(zh * cos + up * sin_lo + dn * sin_hi)
    return jnp.concatenate(outs, axis=1)


def _mixin_kernel(x_ref, sh_ref, sc_ref, g_ref, w_ref, cos_ref, slo_ref, shi_ref,
                  glu_ref, q_ref, k_ref, v_ref, h_scr, a_scr):
    j = pl.program_id(1)

    @pl.when(j == 0)
    def _():
        x = x_ref[...]
        y = x * lax.rsqrt(jnp.mean(x * x, axis=-1, keepdims=True) + EPS)
        h = (y * g_ref[...]) * (1.0 + _rows(sc_ref)) + _rows(sh_ref)
        h_scr[...] = h.astype(BF16)

    z = jnp.dot(h_scr[...], w_ref[...], preferred_element_type=F32)

    @pl.when(j == 0)
    def _():
        a_scr[...] = z

    @pl.when(j == 1)
    def _():
        glu_ref[...] = a_scr[...] * jax.nn.sigmoid(z)

    @pl.when(j == 2)
    def _():
        q_ref[...] = _rope_heads(z, cos_ref[...], slo_ref[...], shi_ref[...])

    @pl.when(j == 3)
    def _():
        k_ref[...] = _rope_heads(z, cos_ref[...], slo_ref[...], shi_ref[...])

    @pl.when(j == 4)
    def _():
        v_ref[...] = z


def _rope_tables(pos):
    inv = ROPE_THETA ** (-np.arange(0, ROT_DIM, 2, dtype=np.float32) / ROT_DIM)
    ang = (pos.astype(np.float32)[:, None] * inv[None, :]).astype(np.float32)
    cos, sin = np.cos(ang), np.sin(ang)
    half = ROT_DIM // 2
    n = pos.shape[0]
    c = np.ones((n, HEAD_DIM), np.float32)
    c[:, :half] = cos
    c[:, half:ROT_DIM] = cos
    s_lo = np.zeros((n, HEAD_DIM), np.float32)
    s_lo[:, :half] = -sin
    s_hi = np.zeros((n, HEAD_DIM), np.float32)
    s_hi[:, half:ROT_DIM] = sin
    return jnp.asarray(c), jnp.asarray(s_lo), jnp.asarray(s_hi)


def _mod_spec(mods, col, tm, group_rows, ngrid):
    if mods.ndim == 3:
        assert group_rows % tm == 0
        per_group = group_rows // tm
        if ngrid == 1:
            return pl.BlockSpec((1, 1, D_MODEL), lambda i: (i // per_group, 0, col))
        return pl.BlockSpec((1, 1, D_MODEL), lambda i, j: (i // per_group, 0, col))
    if ngrid == 1:
        return pl.BlockSpec((tm, D_MODEL), lambda i: (i, col))
    return pl.BlockSpec((tm, D_MODEL), lambda i, j: (i, col))


def _mixin(x, mods, group_rows, norm_g, w_in_bf, tables, tm):
    t = x.shape[0]
    cos, s_lo, s_hi = tables
    pos_blocks = cos.shape[0] // tm
    tab_spec = pl.BlockSpec((tm, HEAD_DIM), lambda i, j: (i % pos_blocks, 0))
    out_spec = pl.BlockSpec((tm, ATT_W), lambda i, j: (i, 0))
    out_sds = jax.ShapeDtypeStruct((t, ATT_W), F32)
    return pl.pallas_call(
        _mixin_kernel,
        grid=(t // tm, 5),
        in_specs=[pl.BlockSpec((tm, D_MODEL), lambda i, j: (i, 0)),
                  _mod_spec(mods, 0, tm, group_rows, 2),
                  _mod_spec(mods, 1, tm, group_rows, 2),
                  pl.BlockSpec((1, D_MODEL), lambda i, j: (0, 0)),
                  pl.BlockSpec((D_MODEL, ATT_W), lambda i, j: (0, j)),
                  tab_spec, tab_spec, tab_spec],
        out_specs=[out_spec, out_spec, out_spec, out_spec],
        out_shape=[out_sds, out_sds, out_sds, out_sds],
        scratch_shapes=[pltpu.VMEM((tm, D_MODEL), BF16), pltpu.VMEM((tm, ATT_W), F32)],
        compiler_params=_params(("arbitrary", "arbitrary"), 56),
        name="mixin",
    )(x, mods, mods, norm_g.reshape(1, D_MODEL), w_in_bf, cos, s_lo, s_hi)


CONV_HIST = CONV_W - 1
CONV_PAD = 32
CONV_ROWS = 64


def _conv_prompt_kernel(g_ref, hist_ref, w_ref, b_ref, o_ref, xp_scr):
    seq = g_ref.shape[1]
    xp_scr[0:CONV_PAD, :] = hist_ref[0]
    xp_scr[CONV_PAD:CONV_PAD + seq, :] = g_ref[0]
    lead = CONV_PAD - CONV_HIST

    def body(c, carry):
        t0 = pl.multiple_of(c * CONV_ROWS, CONV_ROWS)
        win = xp_scr[pl.ds(t0, CONV_ROWS + CONV_PAD), :]
        acc = jnp.zeros((CONV_ROWS, LANES), F32) + b_ref[...]
        for sub in range(SUBLANES):
            shifted = win if sub == 0 else pltpu.roll(win, CONV_ROWS + CONV_PAD - sub, 0)
            for a in range(CONV_PAD // SUBLANES + 1):
                off = a * SUBLANES + sub
                j = off - lead
                if 0 <= j < CONV_W:
                    acc = acc + shifted[a * SUBLANES:a * SUBLANES + CONV_ROWS, :] * w_ref[j:j + 1, :]
        o_ref[0, pl.ds(t0, CONV_ROWS), :] = acc
        return carry

    lax.fori_loop(0, seq // CONV_ROWS, body, 0)


def _conv_prompt(glu3, hist_pad, conv_w, conv_b):
    b, seq, ch = glu3.shape
    return pl.pallas_call(
        _conv_prompt_kernel,
        grid=(b, ch // LANES),
        in_specs=[pl.BlockSpec((1, seq, LANES), lambda i, c: (i, 0, c)),
                  pl.BlockSpec((1, CONV_PAD, LANES), lambda i, c: (i, 0, c)),
                  pl.BlockSpec((CONV_W, LANES), lambda i, c: (0, c)),
                  pl.BlockSpec((1, LANES), lambda i, c: (0, c))],
        out_specs=pl.BlockSpec((1, seq, LANES), lambda i, c: (i, 0, c)),
        out_shape=jax.ShapeDtypeStruct((b, seq, ch), F32),
        scratch_shapes=[pltpu.VMEM((seq + CONV_PAD, LANES), F32)],
        compiler_params=_params(("arbitrary", "arbitrary"), 32),
        name="conv_prompt",
    )(glu3, hist_pad, conv_w, conv_b.reshape(1, ch))


def _conv_sample_kernel(hist_ref, g_ref, wh_ref, wn_ref, b_ref, o_ref):
    hist = hist_ref[...]
    g = g_ref[...]
    n_new = g.shape[1]
    for t in range(n_new):
        y = jnp.sum(hist * wh_ref[t][None], axis=1) + jnp.sum(g * wn_ref[t][None], axis=1)
        o_ref[t] = y + b_ref[...]


def _conv_sample(state, glu3, conv_w, conv_b):
    b, n_new, ch = glu3.shape
    wh = jnp.stack([jnp.pad(conv_w[:CONV_HIST - t], ((t, 0), (0, 0))) for t in range(n_new)])
    wn = jnp.stack([jnp.pad(conv_w[CONV_HIST - t:], ((0, n_new - 1 - t), (0, 0))) for t in range(n_new)])
    nb = 32
    return pl.pallas_call(
        _conv_sample_kernel,
        grid=(b // nb, ch // LANES),
        in_specs=[pl.BlockSpec((nb, CONV_HIST, LANES), lambda i, c: (i, 0, c)),
                  pl.BlockSpec((nb, n_new, LANES), lambda i, c: (i, 0, c)),
                  pl.BlockSpec((n_new, CONV_HIST, LANES), lambda i, c: (0, 0, c)),
                  pl.BlockSpec((n_new, n_new, LANES), lambda i, c: (0, 0, c)),
                  pl.BlockSpec((1, LANES), lambda i, c: (0, c))],
        out_specs=pl.BlockSpec((n_new, nb, LANES), lambda i, c: (0, i, c)),
        out_shape=jax.ShapeDtypeStruct((n_new, b, ch), F32),
        compiler_params=_params(("arbitrary", "arbitrary"), 32),
        name="conv_sample",
    )(state, glu3, wh, wn, conv_b.reshape(1, ch)).transpose(1, 0, 2)


ATTN_UNROLL = 4
MERGE_ROWS = 256


def _head_norm(o, g):
    return o * lax.rsqrt(jnp.mean(o * o, axis=-1, keepdims=True) + EPS) * g


def _attn_prompt_kernel(q_ref, k_ref, v_ref, g_ref, o_ref, acc_scr, m_scr, l_scr):
    seq = q_ref.shape[1]
    qi = lax.broadcasted_iota(jnp.int32, (Q_BLOCK, 2 * Q_BLOCK), 0)
    kj2 = lax.broadcasted_iota(jnp.int32, (Q_BLOCK, 2 * Q_BLOCK), 1)

    for pat, (win, dil) in enumerate(zip(WINDOWS, DILATIONS)):
        assert win // dil == Q_BLOCK
        n_blk = seq // dil // Q_BLOCK

        def rows(ref, start, dil=dil):
            if dil == 1:
                return ref[0, pl.ds(start, Q_BLOCK), :]
            return ref[0, pl.ds(start, Q_BLOCK, stride=dil), :]

        def put(ref, start, val, dil=dil, pat=pat):
            if dil == 1:
                ref[pat, pl.ds(start, Q_BLOCK), :] = val
            else:
                ref[pat, pl.ds(start, Q_BLOCK, stride=dil), :] = val

        def body(i, carry, dil=dil, n_blk=n_blk, rows=rows, put=put):
            r = i // n_blk
            n = i % n_blk
            cur = r + dil * Q_BLOCK * n
            prev = r + dil * Q_BLOCK * jnp.maximum(n - 1, 0)
            if dil == 1:
                cur = pl.multiple_of(cur, Q_BLOCK)
                prev = pl.multiple_of(prev, Q_BLOCK)
            qb = rows(q_ref, cur).astype(BF16)
            k2 = jnp.concatenate([rows(k_ref, prev), rows(k_ref, cur)], axis=0).astype(BF16)
            v2 = jnp.concatenate([rows(v_ref, prev), rows(v_ref, cur)], axis=0).astype(BF16)
            has_prev = jnp.minimum(n, 1)
            lo = qi * has_prev + Q_BLOCK * (1 - has_prev)
            s = _nt_dot(qb, k2) * ATT_SCALE
            s = jnp.where(kj2 >= lo, jnp.where(kj2 <= qi + Q_BLOCK, s, NEG_INF), NEG_INF)
            m = jnp.max(s, axis=-1, keepdims=True)
            p = jnp.exp(s - m).astype(BF16)
            v_ext = jnp.concatenate([v2, jnp.ones_like(v2)], axis=1)
            acc = jnp.dot(p, v_ext, preferred_element_type=F32)
            put(acc_scr, cur, acc[:, :HEAD_DIM])
            put(m_scr, cur, jnp.broadcast_to(m, (Q_BLOCK, HEAD_DIM)))
            put(l_scr, cur, acc[:, HEAD_DIM:])
            return carry

        lax.fori_loop(0, dil * n_blk, body, 0, unroll=ATTN_UNROLL)

    n_pat = len(WINDOWS)

    def merge(c, carry):
        sl = pl.ds(pl.multiple_of(c * MERGE_ROWS, MERGE_ROWS), MERGE_ROWS)
        ms = [m_scr[p, sl, :] for p in range(n_pat)]
        top = functools.reduce(jnp.maximum, ms)
        ws = [jnp.exp(m - top) for m in ms]
        num = functools.reduce(jnp.add, [w * acc_scr[p, sl, :] for p, w in enumerate(ws)])
        den = functools.reduce(jnp.add, [w * l_scr[p, sl, :] for p, w in enumerate(ws)])
        o_ref[0, sl, :] = _head_norm(num / den, g_ref[0])
        return carry

    lax.fori_loop(0, seq // MERGE_ROWS, merge, 0)


def _attn_prompt(q3, k3, v3, head_g):
    b, seq, _ = q3.shape
    spec = pl.BlockSpec((1, seq, HEAD_DIM), lambda i, h: (i, 0, h))
    return pl.pallas_call(
        _attn_prompt_kernel,
        grid=(b, N_HEADS),
        in_specs=[spec, spec, spec, pl.BlockSpec((1, 1, HEAD_DIM), lambda i, h: (h, 0, 0))],
        out_specs=spec,
        out_shape=jax.ShapeDtypeStruct((b, seq, ATT_W), F32),
        scratch_shapes=[pltpu.VMEM((len(WINDOWS), seq, HEAD_DIM), F32)] * 3,
        compiler_params=_params(("arbitrary", "arbitrary"), 32),
        name="attn_prompt",
    )(q3, k3, v3, head_g.reshape(N_HEADS, 1, HEAD_DIM))


S_PAD = 8
TAIL = 512
N_COLS = LANES


def _attn_sample_kernel(q_ref, kn_ref, vn_ref, kt_ref, vt_ref, kd_ref, vd_ref, g_ref, o_ref,
                        k_scr, v_scr, *, n_new, past):
    n_dil = past // DILATIONS[2]
    kd = jnp.swapaxes(kd_ref[0], 0, 1)
    vd = jnp.swapaxes(vd_ref[0], 0, 1)
    kt = jnp.swapaxes(kt_ref[0].reshape(TAIL, N_HEADS, HEAD_DIM), 0, 1)
    vt = jnp.swapaxes(vt_ref[0].reshape(TAIL, N_HEADS, HEAD_DIM), 0, 1)
    for h in range(N_HEADS):
        lanes = slice(h * HEAD_DIM, (h + 1) * HEAD_DIM)
        k_scr[0:TAIL, lanes] = kt[h].astype(BF16)
        v_scr[0:TAIL, lanes] = vt[h].astype(BF16)
        for res in range(n_new):
            dst = slice(TAIL + res * n_dil, TAIL + (res + 1) * n_dil)
            k_scr[dst, lanes] = kd[res * N_HEADS + h].astype(BF16)
            v_scr[dst, lanes] = vd[res * N_HEADS + h].astype(BF16)
    dil0 = TAIL

    q8 = q_ref[0]
    qt = jnp.concatenate([q8] * N_HEADS + [jnp.zeros_like(q8)] * (N_COLS // S_PAD - N_HEADS), axis=0)
    row = lax.broadcasted_iota(jnp.int32, (N_COLS, ATT_W), 0)
    lane = lax.broadcasted_iota(jnp.int32, (N_COLS, ATT_W), 1)
    qbd = jnp.where(row // S_PAD == lane // HEAD_DIM, qt, 0.0).astype(BF16)

    def col_query(shape):
        return lax.broadcasted_iota(jnp.int32, shape, 1) % S_PAD % n_new

    def key_row(shape):
        return lax.broadcasted_iota(jnp.int32, shape, 0)

    groups = []
    s = _nt_dot(k_scr[0:TAIL, :], qbd) * ATT_SCALE
    r, c = key_row(s.shape), col_query(s.shape)
    base = past - TAIL
    mult = ((((base + r) % DILATIONS[1] == (past + c) % DILATIONS[1])
             & (base + r >= past + c - WINDOWS[1])).astype(F32)
            + (base + r >= past + c - WINDOWS[0]).astype(F32))
    groups.append((s, mult, v_scr[0:TAIL, :]))
    for res in range(n_new):
        rows = slice(dil0 + res * n_dil, dil0 + (res + 1) * n_dil)
        s = _nt_dot(k_scr[rows, :], qbd) * ATT_SCALE
        mult = (col_query(s.shape) == res).astype(F32)
        groups.append((s, mult, v_scr[rows, :]))
    s = _nt_dot(kn_ref[0].astype(BF16), qbd) * ATT_SCALE
    r, c = key_row(s.shape), col_query(s.shape)
    mult = (r <= c).astype(F32) + 2.0 * (r == c).astype(F32)
    groups.append((s, mult, vn_ref[0].astype(BF16)))

    m = None
    for s, mult, _ in groups:
        gm = jnp.max(jnp.where(mult > 0, s, NEG_INF), axis=0, keepdims=True)
        m = gm if m is None else jnp.maximum(m, gm)
    num = jnp.zeros((N_COLS, ATT_W), F32)
    den = jnp.zeros((N_COLS, LANES), F32)
    for s, mult, v in groups:
        p = mult * jnp.exp(jnp.where(mult > 0, s - m, NEG_INF))
        pt = p.T.astype(BF16)
        num = num + jnp.dot(pt, v, preferred_element_type=F32)
        den = den + jnp.dot(pt, jnp.ones((v.shape[0], LANES), BF16), preferred_element_type=F32)
    outs = []
    for h in range(N_HEADS):
        o = (num[h * S_PAD:(h + 1) * S_PAD, h * HEAD_DIM:(h + 1) * HEAD_DIM]
             / den[h * S_PAD:(h + 1) * S_PAD, :])
        outs.append(_head_norm(o, g_ref[h]))
    o_ref[0] = jnp.concatenate(outs, axis=1)


def _attn_sample(q8, k8, v8, cache_k, cache_v, head_g, n_new, past):
    b = q8.shape[0]
    lw = cache_k.shape[1]
    assert lw == past and lw % 16 == 0 and lw >= WINDOWS[2] and n_new <= 4
    dil = DILATIONS[2]
    ck_rows = cache_k.reshape(b, lw * N_HEADS, HEAD_DIM)
    cv_rows = cache_v.reshape(b, lw * N_HEADS, HEAD_DIM)
    ck_grp = cache_k.reshape(b, lw // dil, dil * N_HEADS, HEAD_DIM)
    cv_grp = cache_v.reshape(b, lw // dil, dil * N_HEADS, HEAD_DIM)
    new_spec = pl.BlockSpec((1, S_PAD, ATT_W), lambda i: (i, 0, 0))
    tail_spec = pl.BlockSpec((1, TAIL * N_HEADS, HEAD_DIM), lambda i: (i, lw // TAIL - 1, 0))
    dil_spec = pl.BlockSpec((1, lw // dil, n_new * N_HEADS, HEAD_DIM), lambda i: (i, 0, 0, 0))
    n_keys = TAIL + n_new * (lw // dil)
    return pl.pallas_call(
        functools.partial(_attn_sample_kernel, n_new=n_new, past=past),
        grid=(b,),
        in_specs=[new_spec, new_spec, new_spec, tail_spec, tail_spec, dil_spec, dil_spec,
                  pl.BlockSpec((N_HEADS, 1, HEAD_DIM), lambda i: (0, 0, 0))],
        out_specs=new_spec,
        out_shape=jax.ShapeDtypeStruct((b, S_PAD, ATT_W), F32),
        scratch_shapes=[pltpu.VMEM((n_keys, ATT_W), BF16), pltpu.VMEM((n_keys, ATT_W), BF16)],
        compiler_params=_params(("arbitrary",), 48),
        name="attn_sample",
    )(q8, k8, v8, ck_rows, cv_rows, ck_grp, cv_grp, head_g.reshape(N_HEADS, 1, HEAD_DIM))


def _finish_kernel(x_ref, conv_ref, attn_ref, g1_ref, sh2_ref, sc2_ref, lng_ref, lnb_ref, n2g_ref,
                   wout_ref, wq_ref, x1_ref, h2t_ref, qp_ref, cat_scr):
    y = conv_ref[...]
    mu = jnp.mean(y, axis=-1, keepdims=True)
    yc = y - mu
    var = jnp.mean(yc * yc, axis=-1, keepdims=True)
    z = yc * lax.rsqrt(var + EPS) * lng_ref[...] + lnb_ref[...]
    cat_scr[:, :CONV_CH] = (z * jax.nn.sigmoid(z)).astype(BF16)
    cat_scr[:, CONV_CH:] = attn_ref[...].astype(BF16)
    mix = jnp.dot(cat_scr[...], wout_ref[...], preferred_element_type=F32)
    x1 = x_ref[...] + (1.0 + _rows(g1_ref)) * mix
    x1_ref[...] = x1
    n = x1 * lax.rsqrt(jnp.mean(x1 * x1, axis=-1, keepdims=True) + EPS)
    h2 = (n * n2g_ref[...]) * (1.0 + _rows(sc2_ref)) + _rows(sh2_ref)
    h2t_ref[...] = h2.T.astype(BF16)
    qp_ref[...] = jnp.dot(h2.astype(BF16), wq_ref[...], preferred_element_type=F32).astype(BF16)


def _finish(x, conv_raw, attn, mods, group_rows, ln_g, ln_b, n2_g, wout_bf, wq_bf, tm):
    t = x.shape[0]
    row = lambda n: pl.BlockSpec((1, n), lambda i: (0, 0))
    const = lambda shape: pl.BlockSpec(shape, lambda i: (0, 0), pipeline_mode=pl.Buffered(1))
    tile = lambda n: pl.BlockSpec((tm, n), lambda i: (i, 0))
    return pl.pallas_call(
        _finish_kernel,
        grid=(t // tm,),
        in_specs=[tile(D_MODEL), tile(CONV_CH), tile(ATT_W),
                  _mod_spec(mods, 2, tm, group_rows, 1),
                  _mod_spec(mods, 3, tm, group_rows, 1),
                  _mod_spec(mods, 4, tm, group_rows, 1),
                  row(CONV_CH), row(CONV_CH), row(D_MODEL),
                  const((D_MODEL, D_MODEL)), const((D_MODEL, D_MODEL))],
        out_specs=[tile(D_MODEL), pl.BlockSpec((D_MODEL, tm), lambda i: (0, i)), tile(D_MODEL)],
        out_shape=[jax.ShapeDtypeStruct((t, D_MODEL), F32),
                   jax.ShapeDtypeStruct((D_MODEL, t), BF16),
                   jax.ShapeDtypeStruct((t, D_MODEL), BF16)],
        scratch_shapes=[pltpu.VMEM((tm, D_MODEL), BF16)],
        compiler_params=_params(("arbitrary",), 56),
        name="finish",
    )(x, conv_raw, attn, mods, mods, mods, ln_g.reshape(1, CONV_CH), ln_b.reshape(1, CONV_CH),
      n2_g.reshape(1, D_MODEL), wout_bf, wq_bf)


TOPK_RANK = PEER_TOPK + 1
TOPK_ROWS = -(-TOPK_RANK // SUBLANES) * SUBLANES


def _sorting_network(n):
    pairs = []

    def merge(lo, length, r):
        step = 2 * r
        if step < length:
            merge(lo, length, step)
            merge(lo + r, length, step)
            pairs.extend((i, i + r) for i in range(lo + r, lo + length - r, step))
        else:
            pairs.append((lo, lo + r))

    def sort(lo, length):
        if length > 1:
            sort(lo, length // 2)
            sort(lo + length // 2, length // 2)
            merge(lo, length, 1)

    sort(0, n)
    return pairs


def _top_values(s, k, out_scr):
    n_tiles = s.shape[0] // SUBLANES
    v = [s[r * SUBLANES:(r + 1) * SUBLANES, :] for r in range(n_tiles)]
    for i, j in _sorting_network(n_tiles):
        v[i], v[j] = jnp.maximum(v[i], v[j]), jnp.minimum(v[i], v[j])
    out_scr[...] = jnp.full(out_scr.shape, NEG_INF, F32)
    depth = min(n_tiles, k)
    v = v[:depth]
    for i in range(k):
        m = jnp.max(v[0], axis=0, keepdims=True)
        out_scr[i:i + 1, :] = m
        pop = v[0] == m
        live = min(depth, k - i)
        for r in range(live):
            below = v[r + 1] if r + 1 < depth else NEG_INF
            v[r] = jnp.where(pop, below, v[r])


def _pair_candidates(a_scr, b_scr):
    tb = a_scr.shape[1]
    rows_all = lax.broadcasted_iota(jnp.int32, (TOPK_ROWS, tb), 0)
    rows_one = lax.broadcasted_iota(jnp.int32, (SUBLANES, tb), 0)
    cands = [a_scr[0:1, :] + b_scr[...],
             jnp.where(rows_all >= 1, b_scr[0:1, :] + a_scr[...], NEG_INF)]
    single = []
    for i in range(1, TOPK_RANK):
        j_max = TOPK_RANK // (i + 1) - 1
        if j_max >= 2:
            assert j_max < SUBLANES
            ok = (rows_one >= 1) & (rows_one <= j_max)
            cands.append(jnp.where(ok, a_scr[i:i + 1, :] + b_scr[0:SUBLANES, :], NEG_INF))
        elif j_max == 1:
            single.append(i)
    if single:
        assert single == list(range(single[0], single[-1] + 1)) and single[-1] < SUBLANES
        ok = (rows_one >= single[0]) & (rows_one <= single[-1])
        cands.append(jnp.where(ok, b_scr[1:2, :] + a_scr[0:SUBLANES, :], NEG_INF))
    return cands


def _column_reduce(arrays, combine, reduce_rows):
    by_rows = {}
    for a in arrays:
        by_rows[a.shape[0]] = a if a.shape[0] not in by_rows else combine(by_rows[a.shape[0]], a)
    return functools.reduce(combine, [reduce_rows(a, axis=0, keepdims=True) for a in by_rows.values()])


def _topk_kernel(qp_ref, keys_ref, s1_ref, s2_ref, e1_ref, e2_ref, thr_ref, a_scr, b_scr):
    tb = qp_ref.shape[0]
    for h in range(PEER_HEADS):
        sc = []
        for p in range(2):
            col = (h * 2 + p) * PEER_DK
            sc.append(_nt_dot(keys_ref[h, p], qp_ref[:, col:col + PEER_DK]))
        s1, s2 = sc
        _top_values(s1, TOPK_RANK, a_scr)
        _top_values(s2, TOPK_RANK, b_scr)
        cands = _pair_candidates(a_scr, b_scr)
        top = None
        cum = jnp.zeros((1, tb), F32)
        z = jnp.zeros((1, tb), F32)
        v_in = jnp.zeros((1, tb), F32)
        v_out = jnp.zeros((1, tb), F32)
        for _ in range(TOPK_RANK):
            m = _column_reduce(cands, jnp.maximum, jnp.max)
            hits = [c == m for c in cands]
            cnt = _column_reduce([jnp.where(e, 1.0, 0.0) for e in hits], jnp.add, jnp.sum)
            cands = [jnp.where(e, NEG_INF, c) for e, c in zip(hits, cands)]
            top = m if top is None else top
            take = jnp.minimum(cnt, jnp.maximum(PEER_TOPK - cum, 0.0))
            z = z + take * jnp.exp(m - top)
            reached = cum + cnt
            v_in = jnp.where((cum < PEER_TOPK) & (reached >= PEER_TOPK), m, v_in)
            v_out = jnp.where((cum < TOPK_RANK) & (reached >= TOPK_RANK), m, v_out)
            cum = reached
        s1_ref[h] = s1
        s2_ref[h] = s2
        e1_ref[h] = jnp.exp(s1 - a_scr[0:1, :])
        e2_ref[h] = jnp.exp(s2 - b_scr[0:1, :]) / z
        thr_ref[h] = 0.5 * (v_in + v_out)


def _topk(qp, keys_bf, tb):
    t = qp.shape[0]
    big = pl.BlockSpec((PEER_HEADS, PEER_NKEYS, tb), lambda i: (0, 0, i))
    big_sds = jax.ShapeDtypeStruct((PEER_HEADS, PEER_NKEYS, t), F32)
    return pl.pallas_call(
        _topk_kernel,
        grid=(t // tb,),
        in_specs=[pl.BlockSpec((tb, D_MODEL), lambda i: (i, 0)),
                  pl.BlockSpec((PEER_HEADS, 2, PEER_NKEYS, PEER_DK), lambda i: (0, 0, 0, 0))],
        out_specs=[big, big, big, big, pl.BlockSpec((PEER_HEADS, 1, tb), lambda i: (0, 0, i))],
        out_shape=[big_sds, big_sds, big_sds, big_sds, jax.ShapeDtypeStruct((PEER_HEADS, 1, t), F32)],
        scratch_shapes=[pltpu.VMEM((TOPK_ROWS, tb), F32), pltpu.VMEM((TOPK_ROWS, tb), F32)],
        compiler_params=_params(("arbitrary",), 32),
        name="peer_topk",
    )(qp, keys_bf)


_SQRT_HALF = float(np.sqrt(0.5))


def _gelu(x):
    return 0.5 * x * (1.0 + lax.erf(x * _SQRT_HALF))


MXU_TILE = 256
HID_GROUPS = (512, 512)


def _peer_kernel(ht_ref, u_ref, vt_ref, s1_ref, s2_ref, e1_ref, e2_ref, thr_ref, o_ref, acc_scr, *w_refs):
    j = pl.program_id(1)
    tb = ht_ref.shape[1]
    ec = u_ref.shape[0]
    assert ec == SUBLANES * PEER_NKEYS
    n_k, n_n = ec // MXU_TILE, tb // MXU_TILE
    assert len(w_refs) == n_k * n_n

    @pl.when(j == 0)
    def _():
        acc_scr[...] = jnp.zeros_like(acc_scr)

    row0 = pl.multiple_of(j * SUBLANES, SUBLANES)
    assert sum(HID_GROUPS) == ec
    starts = [sum(HID_GROUPS[:g]) for g in range(len(HID_GROUPS))]
    hids = [jnp.dot(u_ref[s:s + n, :], ht_ref[...], preferred_element_type=F32)
            for s, n in zip(starts, HID_GROUPS)]
    per_tile = MXU_TILE // PEER_NKEYS
    for k in range(n_k):
        for n in range(n_n):
            w_ref = w_refs[k * n_n + n]
            for a in range(per_tile):
                ii = k * per_tile + a
                q = max(g for g, s in enumerate(starts) if s <= ii * PEER_NKEYS)
                r = ii * PEER_NKEYS - starts[q]
                for b in range(MXU_TILE // LANES):
                    lc = n * (MXU_TILE // LANES) + b
                    sl = slice(lc * LANES, (lc + 1) * LANES)
                    gate = jnp.zeros((PEER_NKEYS, LANES), F32)
                    for h in range(PEER_HEADS):
                        s1_rows = s1_ref[h, pl.ds(row0, SUBLANES), sl]
                        e1_rows = e1_ref[h, pl.ds(row0, SUBLANES), sl]
                        bound = thr_ref[h, :, sl] - s1_rows[ii:ii + 1, :]
                        val = e2_ref[h, :, sl] * e1_rows[ii:ii + 1, :]
                        gate = gate + jnp.where(s2_ref[h, :, sl] >= bound, val, 0.0)
                    hid = hids[q][r:r + PEER_NKEYS, sl]
                    w_ref[a * PEER_NKEYS:(a + 1) * PEER_NKEYS, b * LANES:(b + 1) * LANES] = (
                        gate * _gelu(hid)).astype(BF16)
    w_all = jnp.concatenate([jnp.concatenate([w_refs[k * n_n + n][...] for n in range(n_n)], axis=1)
                             for k in range(n_k)], axis=0)
    acc_scr[...] += jnp.dot(vt_ref[...], w_all, preferred_element_type=F32)

    @pl.when(j == pl.num_programs(1) - 1)
    def _():
        o_ref[...] = acc_scr[...].T


def _peer(h2t, u_bf, vt_bf, s1, s2, e1, e2, thr, tb, ec):
    t = h2t.shape[1]
    n_chunks = u_bf.shape[0] // ec
    big = pl.BlockSpec((PEER_HEADS, PEER_NKEYS, tb), lambda i, j: (0, 0, i))
    n_tiles = (ec // MXU_TILE) * (tb // MXU_TILE)
    return pl.pallas_call(
        _peer_kernel,
        grid=(t // tb, n_chunks),
        in_specs=[pl.BlockSpec((D_MODEL, tb), lambda i, j: (0, i)),
                  pl.BlockSpec((ec, D_MODEL), lambda i, j: (j, 0)),
                  pl.BlockSpec((D_MODEL, ec), lambda i, j: (0, j)),
                  big, big, big, big,
                  pl.BlockSpec((PEER_HEADS, 1, tb), lambda i, j: (0, 0, i))],
        out_specs=pl.BlockSpec((tb, D_MODEL), lambda i, j: (i, 0)),
        out_shape=jax.ShapeDtypeStruct((t, D_MODEL), F32),
        scratch_shapes=[pltpu.VMEM((D_MODEL, tb), F32)] + [pltpu.VMEM((MXU_TILE, MXU_TILE), BF16)] * n_tiles,
        compiler_params=_params(("arbitrary", "arbitrary"), 56),
        name="peer",
    )(h2t, u_bf, vt_bf, s1, s2, e1, e2, thr)


def _final_kernel(x1_ref, p_ref, g2_ref, fg_ref, y_ref):
    x2 = x1_ref[...] + (1.0 + _rows(g2_ref)) * p_ref[...]
    y_ref[...] = x2 * lax.rsqrt(jnp.mean(x2 * x2, axis=-1, keepdims=True) + EPS) * fg_ref[...]


def _final(x1, peer_out, mods, group_rows, final_g, tm):
    t = x1.shape[0]
    tile = pl.BlockSpec((tm, D_MODEL), lambda i: (i, 0))
    return pl.pallas_call(
        _final_kernel,
        grid=(t // tm,),
        in_specs=[tile, tile, _mod_spec(mods, 5, tm, group_rows, 1),
                  pl.BlockSpec((1, D_MODEL), lambda i: (0, 0))],
        out_specs=tile,
        out_shape=jax.ShapeDtypeStruct((t, D_MODEL), F32),
        compiler_params=_params(("arbitrary",), 48),
        name="final",
    )(x1, peer_out, mods, final_g.reshape(1, D_MODEL))


PAST_LEN = 2048
MIXIN_TOKENS = 512
TOKEN_TILE = 256
PEER_TOKENS = 512
PEER_CHUNK = 1024
TOPK_TOKENS = 256


def _channel_tail(x, conv_raw, attn, mods, group_rows, weights):
    x1, h2t, qp = _finish(x, conv_raw, attn, mods, group_rows, weights["ln_g"], weights["ln_b"],
                          weights["n2_g"], weights["wout"], weights["wq"], TOKEN_TILE)
    s1, s2, e1, e2, thr = _topk(qp, weights["keys"], TOPK_TOKENS)
    peer_out = _peer(h2t, weights["u"], weights["vt"], s1, s2, e1, e2, thr, PEER_TOKENS, PEER_CHUNK)
    return _final(x1, peer_out, mods, group_rows, weights["final_g"], TOKEN_TILE)


def kernel(x_prompt, x_sample, cache_k, cache_v, state_conv, c_prompt, c_sample, w_ada, b_ada, norm1_g, w_in,
           conv_w, conv_b, conv_ln_g, conv_ln_b, attn_out_g, w_out, norm2_g, peer_wq, peer_keys, peer_u, peer_v,
           final_g):
    b, seq, d = x_prompt.shape
    db, n_new, _ = x_sample.shape
    depth = w_ada.shape[0]
    assert depth == 1 and d == D_MODEL
    cache_len = cache_k.shape[2]
    n_sample = db * n_new
    assert seq % MIXIN_TOKENS == 0 and n_sample % PEER_TOKENS == 0

    c_rows = jnp.concatenate([c_prompt, jnp.repeat(c_sample, n_new, axis=0)], axis=0)
    mods = _ada(c_rows, w_ada[0], b_ada[0])
    mods_p = mods[:b].reshape(b, 1, N_ADA * D_MODEL)
    mods_s = mods[b:]

    weights = {
        "ln_g": conv_ln_g[0], "ln_b": conv_ln_b[0], "n2_g": norm2_g[0],
        "wout": w_out[0].astype(BF16), "wq": peer_wq[0].astype(BF16),
        "keys": peer_keys[0].astype(BF16), "u": peer_u[0].astype(BF16),
        "vt": peer_v[0].T.astype(BF16), "final_g": final_g,
    }
    w_in_bf = w_in[0].astype(BF16)

    xp = x_prompt.reshape(b * seq, d)
    tables_p = _rope_tables(np.arange(seq))
    glu, q, k, v = _mixin(xp, mods_p, seq, norm1_g[0], w_in_bf, tables_p, MIXIN_TOKENS)
    glu3 = glu.reshape(b, seq, CONV_CH)
    conv_raw = _conv_prompt(glu3, jnp.zeros((b, CONV_PAD, CONV_CH), F32), conv_w[0], conv_b[0])
    attn = _attn_prompt(q.reshape(b, seq, ATT_W), k.reshape(b, seq, ATT_W), v.reshape(b, seq, ATT_W),
                        attn_out_g[0])
    y_prompt = _channel_tail(xp, conv_raw.reshape(b * seq, CONV_CH), attn.reshape(b * seq, ATT_W),
                             mods_p, seq, weights).reshape(b, seq, d)
    keep = min(WINDOWS[2], seq)
    new_k_prompt = k.reshape(b, seq, N_HEADS, HEAD_DIM)[:, seq - keep:][None]
    new_v_prompt = v.reshape(b, seq, N_HEADS, HEAD_DIM)[:, seq - keep:][None]
    new_conv_prompt = glu3[:, seq - CONV_HIST:][None]

    xs = x_sample.reshape(n_sample, d)
    tables_s = tuple(jnp.tile(t, (TOKEN_TILE // n_new, 1)) for t in _rope_tables(PAST_LEN + np.arange(n_new)))
    glu_s, q_s, k_s, v_s = _mixin(xs, mods_s, None, norm1_g[0], w_in_bf, tables_s, TOKEN_TILE)
    glu_s3 = glu_s.reshape(db, n_new, CONV_CH)
    conv_raw_s = _conv_sample(state_conv[0], glu_s3, conv_w[0], conv_b[0])
    pad8 = lambda a: jnp.pad(a.reshape(db, n_new, ATT_W), ((0, 0), (0, S_PAD - n_new), (0, 0)))
    attn_s = _attn_sample(pad8(q_s), pad8(k_s), pad8(v_s), cache_k[0], cache_v[0], attn_out_g[0], n_new,
                          cache_len)
    attn_s = attn_s[:, :n_new].reshape(n_sample, ATT_W)
    y_sample = _channel_tail(xs, conv_raw_s.reshape(n_sample, CONV_CH), attn_s,
                             mods_s, None, weights).reshape(db, n_new, d)
    new_k_sample = k_s.reshape(db, n_new, N_HEADS, HEAD_DIM)[None]
    new_v_sample = v_s.reshape(db, n_new, N_HEADS, HEAD_DIM)[None]
    new_conv_sample = jnp.concatenate([state_conv[0], glu_s3], axis=1)[:, n_new:][None]

    return (y_prompt, y_sample, new_k_prompt, new_v_prompt, new_conv_prompt,
            new_k_sample, new_v_sample, new_conv_sample)
```

```python
import functools

import numpy as np
import jax
import jax.numpy as jnp
from jax import lax
from jax.experimental import pallas as pl
from jax.experimental.pallas import tpu as pltpu

F32 = jnp.float32
BF16 = jnp.bfloat16

D_MODEL = 2048
CONV_CH = 1024
CONV_W = 31
N_HEADS = 8
HEAD_DIM = 128
ATT_W = N_HEADS * HEAD_DIM
ROT_DIM = HEAD_DIM // 4
ROPE_THETA = 500000.0
WINDOWS = (128, 512, 2048)
DILATIONS = (1, 4, 16)
Q_BLOCK = 128
ATT_SCALE = HEAD_DIM ** -0.5
N_ADA = 6
PEER_HEADS = 8
PEER_NKEYS = 128
PEER_DK = 128
PEER_TOPK = 16
EPS = 1e-6

LANES = 128
SUBLANES = 8
MIB = 1024 * 1024

NEG_INF = float("-inf")


def _nt_dot(a, b):
    return lax.dot_general(a, b, (((1,), (1,)), ((), ())), preferred_element_type=F32)


def _rows(ref):
    return ref[0] if len(ref.shape) == 3 else ref[...]


def _params(sem, vmem_mib):
    return pltpu.CompilerParams(dimension_semantics=sem, vmem_limit_bytes=vmem_mib * MIB)


def _ada_kernel(c_ref, w_ref, b_ref, o_ref):
    c = c_ref[...]
    a = (c * jax.nn.sigmoid(c)).astype(BF16)
    o_ref[...] = jnp.dot(a, w_ref[...].astype(BF16), preferred_element_type=F32) + b_ref[...]


def _ada(c_rows, w_ada, b_ada):
    rows = c_rows.shape[0]
    n = w_ada.shape[1]
    tn = 1024
    return pl.pallas_call(
        _ada_kernel,
        grid=(n // tn,),
        in_specs=[pl.BlockSpec((rows, D_MODEL), lambda j: (0, 0)),
                  pl.BlockSpec((D_MODEL, tn), lambda j: (0, j)),
                  pl.BlockSpec((1, tn), lambda j: (0, j))],
        out_specs=pl.BlockSpec((rows, tn), lambda j: (0, j)),
        out_shape=jax.ShapeDtypeStruct((rows, n), F32),
        compiler_params=_params(("arbitrary",), 48),
        name="ada",
    )(c_rows, w_ada, b_ada.reshape(1, n))


def _rope_heads(z, cos, sin_lo, sin_hi):
    outs = []
    for h in range(N_HEADS):
        zh = z[:, h * HEAD_DIM:(h + 1) * HEAD_DIM]
        up = pltpu.roll(zh, HEAD_DIM - ROT_DIM // 2, 1)
        dn = pltpu.roll(zh, ROT_DIM // 2, 1)
        outs.append(zh * cos + up * sin_lo + dn * sin_hi)
    return jnp.concatenate(outs, axis=1)


def _mixin_kernel(x_ref, sh_ref, sc_ref, g_ref, w_ref, cos_ref, slo_ref, shi_ref,
                  glu_ref, q_ref, k_ref, v_ref, h_scr, a_scr):
    j = pl.program_id(1)

    @pl.when(j == 0)
    def _():
        x = x_ref[...]
        y = x * lax.rsqrt(jnp.mean(x * x, axis=-1, keepdims=True) + EPS)
        h = (y * g_ref[...]) * (1.0 + _rows(sc_ref)) + _rows(sh_ref)
        h_scr[...] = h.astype(BF16)

    z = jnp.dot(h_scr[...], w_ref[...], preferred_element_type=F32)

    @pl.when(j == 0)
    def _():
        a_scr[...] = z

    @pl.when(j == 1)
    def _():
        glu_ref[...] = a_scr[...] * jax.nn.sigmoid(z)

    @pl.when(j == 2)
    def _():
        q_ref[...] = _rope_heads(z, cos_ref[...], slo_ref[...], shi_ref[...])

    @pl.when(j == 3)
    def _():
        k_ref[...] = _rope_heads(z, cos_ref[...], slo_ref[...], shi_ref[...])

    @pl.when(j == 4)
    def _():
        v_ref[...] = z


def _rope_tables(pos):
    inv = ROPE_THETA ** (-np.arange(0, ROT_DIM, 2, dtype=np.float32) / ROT_DIM)
    ang = (pos.astype(np.float32)[:, None] * inv[None, :]).astype(np.float32)
    cos, sin = np.cos(ang), np.sin(ang)
    half = ROT_DIM // 2
    n = pos.shape[0]
    c = np.ones((n, HEAD_DIM), np.float32)
    c[:, :half] = cos
    c[:, half:ROT_DIM] = cos
    s_lo = np.zeros((n, HEAD_DIM), np.float32)
    s_lo[:, :half] = -sin
    s_hi = np.zeros((n, HEAD_DIM), np.float32)
    s_hi[:, half:ROT_DIM] = sin
    return jnp.asarray(c), jnp.asarray(s_lo), jnp.asarray(s_hi)


def _mod_spec(mods, col, tm, group_rows, ngrid):
    if mods.ndim == 3:
        assert group_rows % tm == 0
        per_group = group_rows // tm
        if ngrid == 1:
            return pl.BlockSpec((1, 1, D_MODEL), lambda i: (i // per_group, 0, col))
        return pl.BlockSpec((1, 1, D_MODEL), lambda i, j: (i // per_group, 0, col))
    if ngrid == 1:
        return pl.BlockSpec((tm, D_MODEL), lambda i: (i, col))
    return pl.BlockSpec((tm, D_MODEL), lambda i, j: (i, col))


def _mixin(x, mods, group_rows, norm_g, w_in_bf, tables, tm):
    t = x.shape[0]
    cos, s_lo, s_hi = tables
    pos_blocks = cos.shape[0] // tm
    tab_spec = pl.BlockSpec((tm, HEAD_DIM), lambda i, j: (i % pos_blocks, 0))
    out_spec = pl.BlockSpec((tm, ATT_W), lambda i, j: (i, 0))
    out_sds = jax.ShapeDtypeStruct((t, ATT_W), F32)
    return pl.pallas_call(
        _mixin_kernel,
        grid=(t // tm, 5),
        in_specs=[pl.BlockSpec((tm, D_MODEL), lambda i, j: (i, 0)),
                  _mod_spec(mods, 0, tm, group_rows, 2),
                  _mod_spec(mods, 1, tm, group_rows, 2),
                  pl.BlockSpec((1, D_MODEL), lambda i, j: (0, 0)),
                  pl.BlockSpec((D_MODEL, ATT_W), lambda i, j: (0, j)),
                  tab_spec, tab_spec, tab_spec],
        out_specs=[out_spec, out_spec, out_spec, out_spec],
        out_shape=[out_sds, out_sds, out_sds, out_sds],
        scratch_shapes=[pltpu.VMEM((tm, D_MODEL), BF16), pltpu.VMEM((tm, ATT_W), F32)],
        compiler_params=_params(("arbitrary", "arbitrary"), 56),
        name="mixin",
    )(x, mods, mods, norm_g.reshape(1, D_MODEL), w_in_bf, cos, s_lo, s_hi)


CONV_HIST = CONV_W - 1
CONV_PAD = 32
CONV_ROWS = 64


def _conv_prompt_kernel(g_ref, hist_ref, w_ref, b_ref, o_ref, xp_scr):
    seq = g_ref.shape[1]
    xp_scr[0:CONV_PAD, :] = hist_ref[0]
    xp_scr[CONV_PAD:CONV_PAD + seq, :] = g_ref[0]
    lead = CONV_PAD - CONV_HIST

    def body(c, carry):
        t0 = pl.multiple_of(c * CONV_ROWS, CONV_ROWS)
        win = xp_scr[pl.ds(t0, CONV_ROWS + CONV_PAD), :]
        acc = jnp.zeros((CONV_ROWS, LANES), F32) + b_ref[...]
        for sub in range(SUBLANES):
            shifted = win if sub == 0 else pltpu.roll(win, CONV_ROWS + CONV_PAD - sub, 0)
            for a in range(CONV_PAD // SUBLANES + 1):
                off = a * SUBLANES + sub
                j = off - lead
                if 0 <= j < CONV_W:
                    acc = acc + shifted[a * SUBLANES:a * SUBLANES + CONV_ROWS, :] * w_ref[j:j + 1, :]
        o_ref[0, pl.ds(t0, CONV_ROWS), :] = acc
        return carry

    lax.fori_loop(0, seq // CONV_ROWS, body, 0)


def _conv_prompt(glu3, hist_pad, conv_w, conv_b):
    b, seq, ch = glu3.shape
    return pl.pallas_call(
        _conv_prompt_kernel,
        grid=(b, ch // LANES),
        in_specs=[pl.BlockSpec((1, seq, LANES), lambda i, c: (i, 0, c)),
                  pl.BlockSpec((1, CONV_PAD, LANES), lambda i, c: (i, 0, c)),
                  pl.BlockSpec((CONV_W, LANES), lambda i, c: (0, c)),
                  pl.BlockSpec((1, LANES), lambda i, c: (0, c))],
        out_specs=pl.BlockSpec((1, seq, LANES), lambda i, c: (i, 0, c)),
        out_shape=jax.ShapeDtypeStruct((b, seq, ch), F32),
        scratch_shapes=[pltpu.VMEM((seq + CONV_PAD, LANES), F32)],
        compiler_params=_params(("arbitrary", "arbitrary"), 32),
        name="conv_prompt",
    )(glu3, hist_pad, conv_w, conv_b.reshape(1, ch))


def _conv_sample_kernel(hist_ref, g_ref, wh_ref, wn_ref, b_ref, o_ref):
    hist = hist_ref[...]
    g = g_ref[...]
    n_new = g.shape[1]
    for t in range(n_new):
        y = jnp.sum(hist * wh_ref[t][None], axis=1) + jnp.sum(g * wn_ref[t][None], axis=1)
        o_ref[t] = y + b_ref[...]


def _conv_sample(state, glu3, conv_w, conv_b):
    b, n_new, ch = glu3.shape
    wh = jnp.stack([jnp.pad(conv_w[:CONV_HIST - t], ((t, 0), (0, 0))) for t in range(n_new)])
    wn = jnp.stack([jnp.pad(conv_w[CONV_HIST - t:], ((0, n_new - 1 - t), (0, 0))) for t in range(n_new)])
    nb = 32
    return pl.pallas_call(
        _conv_sample_kernel,
        grid=(b // nb, ch // LANES),
        in_specs=[pl.BlockSpec((nb, CONV_HIST, LANES), lambda i, c: (i, 0, c)),
                  pl.BlockSpec((nb, n_new, LANES), lambda i, c: (i, 0, c)),
                  pl.BlockSpec((n_new, CONV_HIST, LANES), lambda i, c: (0, 0, c)),
                  pl.BlockSpec((n_new, n_new, LANES), lambda i, c: (0, 0, c)),
                  pl.BlockSpec((1, LANES), lambda i, c: (0, c))],
        out_specs=pl.BlockSpec((n_new, nb, LANES), lambda i, c: (0, i, c)),
        out_shape=jax.ShapeDtypeStruct((n_new, b, ch), F32),
        compiler_params=_params(("arbitrary", "arbitrary"), 32),
        name="conv_sample",
    )(state, glu3, wh, wn, conv_b.reshape(1, ch)).transpose(1, 0, 2)


ATTN_UNROLL = 16
MERGE_ROWS = 256


def _head_norm(o, g):
    return o * lax.rsqrt(jnp.mean(o * o, axis=-1, keepdims=True) + EPS) * g


def _attn_prompt_kernel(q_ref, k_ref, v_ref, g_ref, o_ref, acc_scr, m_scr, l_scr):
    seq = q_ref.shape[1]
    qi = lax.broadcasted_iota(jnp.int32, (Q_BLOCK, 2 * Q_BLOCK), 0)
    kj2 = lax.broadcasted_iota(jnp.int32, (Q_BLOCK, 2 * Q_BLOCK), 1)

    for pat, (win, dil) in enumerate(zip(WINDOWS, DILATIONS)):
        assert win // dil == Q_BLOCK
        n_blk = seq // dil // Q_BLOCK

        def rows(ref, start, dil=dil):
            if dil == 1:
                return ref[0, pl.ds(start, Q_BLOCK), :]
            return ref[0, pl.ds(start, Q_BLOCK, stride=dil), :]

        def put(ref, start, val, dil=dil, pat=pat):
            if dil == 1:
                ref[pat, pl.ds(start, Q_BLOCK), :] = val
            else:
                ref[pat, pl.ds(start, Q_BLOCK, stride=dil), :] = val

        def body(i, carry, dil=dil, n_blk=n_blk, rows=rows, put=put):
            r = i // n_blk
            n = i % n_blk
            cur = r + dil * Q_BLOCK * n
            prev = r + dil * Q_BLOCK * jnp.maximum(n - 1, 0)
            if dil == 1:
                cur = pl.multiple_of(cur, Q_BLOCK)
                prev = pl.multiple_of(prev, Q_BLOCK)
            qb = rows(q_ref, cur).astype(BF16)
            k2 = jnp.concatenate([rows(k_ref, prev), rows(k_ref, cur)], axis=0).astype(BF16)
            v2 = jnp.concatenate([rows(v_ref, prev), rows(v_ref, cur)], axis=0).astype(BF16)
            has_prev = jnp.minimum(n, 1)
            lo = qi * has_prev + Q_BLOCK * (1 - has_prev)
            s = _nt_dot(qb, k2) * ATT_SCALE
            s = jnp.where(kj2 >= lo, jnp.where(kj2 <= qi + Q_BLOCK, s, NEG_INF), NEG_INF)
            m = jnp.max(s, axis=-1, keepdims=True)
            p = jnp.exp(s - m).astype(BF16)
            v_ext = jnp.concatenate([v2, jnp.ones_like(v2)], axis=1)
            acc = jnp.dot(p, v_ext, preferred_element_type=F32)
            put(acc_scr, cur, acc[:, :HEAD_DIM])
            put(m_scr, cur, jnp.broadcast_to(m, (Q_BLOCK, HEAD_DIM)))
            put(l_scr, cur, acc[:, HEAD_DIM:])
            return carry

        lax.fori_loop(0, dil * n_blk, body, 0, unroll=ATTN_UNROLL)

    n_pat = len(WINDOWS)

    def merge(c, carry):
        sl = pl.ds(pl.multiple_of(c * MERGE_ROWS, MERGE_ROWS), MERGE_ROWS)
        ms = [m_scr[p, sl, :] for p in range(n_pat)]
        top = functools.reduce(jnp.maximum, ms)
        ws = [jnp.exp(m - top) for m in ms]
        num = functools.reduce(jnp.add, [w * acc_scr[p, sl, :] for p, w in enumerate(ws)])
        den = functools.reduce(jnp.add, [w * l_scr[p, sl, :] for p, w in enumerate(ws)])
        o_ref[0, sl, :] = _head_norm(num / den, g_ref[0])
        return carry

    lax.fori_loop(0, seq // MERGE_ROWS, merge, 0)


def _attn_prompt(q3, k3, v3, head_g):
    b, seq, _ = q3.shape
    spec = pl.BlockSpec((1, seq, HEAD_DIM), lambda i, h: (i, 0, h))
    return pl.pallas_call(
        _attn_prompt_kernel,
        grid=(b, N_HEADS),
        in_specs=[spec, spec, spec, pl.BlockSpec((1, 1, HEAD_DIM), lambda i, h: (h, 0, 0))],
        out_specs=spec,
        out_shape=jax.ShapeDtypeStruct((b, seq, ATT_W), F32),
        scratch_shapes=[pltpu.VMEM((len(WINDOWS), seq, HEAD_DIM), F32)] * 3,
        compiler_params=_params(("arbitrary", "arbitrary"), 32),
        name="attn_prompt",
    )(q3, k3, v3, head_g.reshape(N_HEADS, 1, HEAD_DIM))


S_PAD = 8
TAIL = 512
N_COLS = LANES


def _attn_sample_kernel(q_ref, kn_ref, vn_ref, kt_ref, vt_ref, kd_ref, vd_ref, g_ref, o_ref,
                        k_scr, v_scr, *, n_new, past):
    n_dil = past // DILATIONS[2]
    kd = jnp.swapaxes(kd_ref[0], 0, 1)
    vd = jnp.swapaxes(vd_ref[0], 0, 1)
    kt = jnp.swapaxes(kt_ref[0].reshape(TAIL, N_HEADS, HEAD_DIM), 0, 1)
    vt = jnp.swapaxes(vt_ref[0].reshape(TAIL, N_HEADS, HEAD_DIM), 0, 1)
    for h in range(N_HEADS):
        lanes = slice(h * HEAD_DIM, (h + 1) * HEAD_DIM)
        k_scr[0:TAIL, lanes] = kt[h].astype(BF16)
        v_scr[0:TAIL, lanes] = vt[h].astype(BF16)
        for res in range(n_new):
            dst = slice(TAIL + res * n_dil, TAIL + (res + 1) * n_dil)
            k_scr[dst, lanes] = kd[res * N_HEADS + h].astype(BF16)
            v_scr[dst, lanes] = vd[res * N_HEADS + h].astype(BF16)
    dil0 = TAIL

    q8 = q_ref[0]
    qt = jnp.concatenate([q8] * N_HEADS + [jnp.zeros_like(q8)] * (N_COLS // S_PAD - N_HEADS), axis=0)
    row = lax.broadcasted_iota(jnp.int32, (N_COLS, ATT_W), 0)
    lane = lax.broadcasted_iota(jnp.int32, (N_COLS, ATT_W), 1)
    qbd = jnp.where(row // S_PAD == lane // HEAD_DIM, qt, 0.0).astype(BF16)

    def col_query(shape):
        return lax.broadcasted_iota(jnp.int32, shape, 1) % S_PAD % n_new

    def key_row(shape):
        return lax.broadcasted_iota(jnp.int32, shape, 0)

    groups = []
    s = _nt_dot(k_scr[0:TAIL, :], qbd) * ATT_SCALE
    r, c = key_row(s.shape), col_query(s.shape)
    base = past - TAIL
    mult = ((((base + r) % DILATIONS[1] == (past + c) % DILATIONS[1])
             & (base + r >= past + c - WINDOWS[1])).astype(F32)
            + (base + r >= past + c - WINDOWS[0]).astype(F32))
    groups.append((s, mult, v_scr[0:TAIL, :]))
    for res in range(n_new):
        rows = slice(dil0 + res * n_dil, dil0 + (res + 1) * n_dil)
        s = _nt_dot(k_scr[rows, :], qbd) * ATT_SCALE
        mult = (col_query(s.shape) == res).astype(F32)
        groups.append((s, mult, v_scr[rows, :]))
    s = _nt_dot(kn_ref[0].astype(BF16), qbd) * ATT_SCALE
    r, c = key_row(s.shape), col_query(s.shape)
    mult = (r <= c).astype(F32) + 2.0 * (r == c).astype(F32)
    groups.append((s, mult, vn_ref[0].astype(BF16)))

    m = None
    for s, mult, _ in groups:
        gm = jnp.max(jnp.where(mult > 0, s, NEG_INF), axis=0, keepdims=True)
        m = gm if m is None else jnp.maximum(m, gm)
    num = jnp.zeros((N_COLS, ATT_W), F32)
    den = jnp.zeros((N_COLS, LANES), F32)
    for s, mult, v in groups:
        p = mult * jnp.exp(jnp.where(mult > 0, s - m, NEG_INF))
        pt = p.T.astype(BF16)
        num = num + jnp.dot(pt, v, preferred_element_type=F32)
        den = den + jnp.dot(pt, jnp.ones((v.shape[0], LANES), BF16), preferred_element_type=F32)
    outs = []
    for h in range(N_HEADS):
        o = (num[h * S_PAD:(h + 1) * S_PAD, h * HEAD_DIM:(h + 1) * HEAD_DIM]
             / den[h * S_PAD:(h + 1) * S_PAD, :])
        outs.append(_head_norm(o, g_ref[h]))
    o_ref[0] = jnp.concatenate(outs, axis=1)


def _attn_sample(q8, k8, v8, cache_k, cache_v, head_g, n_new, past):
    b = q8.shape[0]
    lw = cache_k.shape[1]
    assert lw == past and lw % 16 == 0 and lw >= WINDOWS[2] and n_new <= 4
    dil = DILATIONS[2]
    ck_rows = cache_k.reshape(b, lw * N_HEADS, HEAD_DIM)
    cv_rows = cache_v.reshape(b, lw * N_HEADS, HEAD_DIM)
    ck_grp = cache_k.reshape(b, lw // dil, dil * N_HEADS, HEAD_DIM)
    cv_grp = cache_v.reshape(b, lw // dil, dil * N_HEADS, HEAD_DIM)
    new_spec = pl.BlockSpec((1, S_PAD, ATT_W), lambda i: (i, 0, 0))
    tail_spec = pl.BlockSpec((1, TAIL * N_HEADS, HEAD_DIM), lambda i: (i, lw // TAIL - 1, 0))
    dil_spec = pl.BlockSpec((1, lw // dil, n_new * N_HEADS, HEAD_DIM), lambda i: (i, 0, 0, 0))
    n_keys = TAIL + n_new * (lw // dil)
    return pl.pallas_call(
        functools.partial(_attn_sample_kernel, n_new=n_new, past=past),
        grid=(b,),
        in_specs=[new_spec, new_spec, new_spec, tail_spec, tail_spec, dil_spec, dil_spec,
                  pl.BlockSpec((N_HEADS, 1, HEAD_DIM), lambda i: (0, 0, 0))],
        out_specs=new_spec,
        out_shape=jax.ShapeDtypeStruct((b, S_PAD, ATT_W), F32),
        scratch_shapes=[pltpu.VMEM((n_keys, ATT_W), BF16), pltpu.VMEM((n_keys, ATT_W), BF16)],
        compiler_params=_params(("arbitrary",), 48),
        name="attn_sample",
    )(q8, k8, v8, ck_rows, cv_rows, ck_grp, cv_grp, head_g.reshape(N_HEADS, 1, HEAD_DIM))


def _finish_kernel(x_ref, conv_ref, attn_ref, g1_ref, sh2_ref, sc2_ref, lng_ref, lnb_ref, n2g_ref,
                   wout_ref, wq_ref, x1_ref, h2t_ref, qp_ref, cat_scr):
    y = conv_ref[...]
    mu = jnp.mean(y, axis=-1, keepdims=True)
    yc = y - mu
    var = jnp.mean(yc * yc, axis=-1, keepdims=True)
    z = yc * lax.rsqrt(var + EPS) * lng_ref[...] + lnb_ref[...]
    cat_scr[:, :CONV_CH] = (z * jax.nn.sigmoid(z)).astype(BF16)
    cat_scr[:, CONV_CH:] = attn_ref[...].astype(BF16)
    mix = jnp.dot(cat_scr[...], wout_ref[...], preferred_element_type=F32)
    x1 = x_ref[...] + (1.0 + _rows(g1_ref)) * mix
    x1_ref[...] = x1
    n = x1 * lax.rsqrt(jnp.mean(x1 * x1, axis=-1, keepdims=True) + EPS)
    h2 = (n * n2g_ref[...]) * (1.0 + _rows(sc2_ref)) + _rows(sh2_ref)
    h2t_ref[...] = h2.T.astype(BF16)
    qp_ref[...] = jnp.dot(h2.astype(BF16), wq_ref[...], preferred_element_type=F32).astype(BF16)


def _finish(x, conv_raw, attn, mods, group_rows, ln_g, ln_b, n2_g, wout_bf, wq_bf, tm):
    t = x.shape[0]
    row = lambda n: pl.BlockSpec((1, n), lambda i: (0, 0))
    const = lambda shape: pl.BlockSpec(shape, lambda i: (0, 0), pipeline_mode=pl.Buffered(1))
    tile = lambda n: pl.BlockSpec((tm, n), lambda i: (i, 0))
    return pl.pallas_call(
        _finish_kernel,
        grid=(t // tm,),
        in_specs=[tile(D_MODEL), tile(CONV_CH), tile(ATT_W),
                  _mod_spec(mods, 2, tm, group_rows, 1),
                  _mod_spec(mods, 3, tm, group_rows, 1),
                  _mod_spec(mods, 4, tm, group_rows, 1),
                  row(CONV_CH), row(CONV_CH), row(D_MODEL),
                  const((D_MODEL, D_MODEL)), const((D_MODEL, D_MODEL))],
        out_specs=[tile(D_MODEL), pl.BlockSpec((D_MODEL, tm), lambda i: (0, i)), tile(D_MODEL)],
        out_shape=[jax.ShapeDtypeStruct((t, D_MODEL), F32),
                   jax.ShapeDtypeStruct((D_MODEL, t), BF16),
                   jax.ShapeDtypeStruct((t, D_MODEL), BF16)],
        scratch_shapes=[pltpu.VMEM((tm, D_MODEL), BF16)],
        compiler_params=_params(("arbitrary",), 56),
        name="finish",
    )(x, conv_raw, attn, mods, mods, mods, ln_g.reshape(1, CONV_CH), ln_b.reshape(1, CONV_CH),
      n2_g.reshape(1, D_MODEL), wout_bf, wq_bf)


TOPK_RANK = PEER_TOPK + 1
TOPK_ROWS = -(-TOPK_RANK // SUBLANES) * SUBLANES


def _sorting_network(n):
    pairs = []

    def merge(lo, length, r):
        step = 2 * r
        if step < length:
            merge(lo, length, step)
            merge(lo + r, length, step)
            pairs.extend((i, i + r) for i in range(lo + r, lo + length - r, step))
        else:
            pairs.append((lo, lo + r))

    def sort(lo, length):
        if length > 1:
            sort(lo, length // 2)
            sort(lo + length // 2, length // 2)
            merge(lo, length, 1)

    sort(0, n)
    return pairs


def _top_values(s, k, out_scr):
    n_tiles = s.shape[0] // SUBLANES
    v = [s[r * SUBLANES:(r + 1) * SUBLANES, :] for r in range(n_tiles)]
    for i, j in _sorting_network(n_tiles):
        v[i], v[j] = jnp.maximum(v[i], v[j]), jnp.minimum(v[i], v[j])
    out_scr[...] = jnp.full(out_scr.shape, NEG_INF, F32)
    depth = min(n_tiles, k)
    v = v[:depth]
    for i in range(k):
        m = jnp.max(v[0], axis=0, keepdims=True)
        out_scr[i:i + 1, :] = m
        pop = v[0] == m
        live = min(depth, k - i)
        for r in range(live):
            below = v[r + 1] if r + 1 < depth else NEG_INF
            v[r] = jnp.where(pop, below, v[r])


def _pair_candidates(a_scr, b_scr):
    tb = a_scr.shape[1]
    rows_all = lax.broadcasted_iota(jnp.int32, (TOPK_ROWS, tb), 0)
    rows_one = lax.broadcasted_iota(jnp.int32, (SUBLANES, tb), 0)
    cands = [a_scr[0:1, :] + b_scr[...],
             jnp.where(rows_all >= 1, b_scr[0:1, :] + a_scr[...], NEG_INF)]
    single = []
    for i in range(1, TOPK_RANK):
        j_max = TOPK_RANK // (i + 1) - 1
        if j_max >= 2:
            assert j_max < SUBLANES
            ok = (rows_one >= 1) & (rows_one <= j_max)
            cands.append(jnp.where(ok, a_scr[i:i + 1, :] + b_scr[0:SUBLANES, :], NEG_INF))
        elif j_max == 1:
            single.append(i)
    if single:
        assert single == list(range(single[0], single[-1] + 1)) and single[-1] < SUBLANES
        ok = (rows_one >= single[0]) & (rows_one <= single[-1])
        cands.append(jnp.where(ok, b_scr[1:2, :] + a_scr[0:SUBLANES, :], NEG_INF))
    return cands


def _column_reduce(arrays, combine, reduce_rows):
    by_rows = {}
    for a in arrays:
        by_rows[a.shape[0]] = a if a.shape[0] not in by_rows else combine(by_rows[a.shape[0]], a)
    return functools.reduce(combine, [reduce_rows(a, axis=0, keepdims=True) for a in by_rows.values()])


def _topk_kernel(qp_ref, keys_ref, s1_ref, s2_ref, e1_ref, e2_ref, thr_ref, a_scr, b_scr):
    tb = qp_ref.shape[0]
    for h in range(PEER_HEADS):
        sc = []
        for p in range(2):
            col = (h * 2 + p) * PEER_DK
            sc.append(_nt_dot(keys_ref[h, p], qp_ref[:, col:col + PEER_DK]))
        s1, s2 = sc
        _top_values(s1, TOPK_RANK, a_scr)
        _top_values(s2, TOPK_RANK, b_scr)
        cands = _pair_candidates(a_scr, b_scr)
        top = None
        cum = jnp.zeros((1, tb), F32)
        z = jnp.zeros((1, tb), F32)
        v_in = jnp.zeros((1, tb), F32)
        v_out = jnp.zeros((1, tb), F32)
        for _ in range(TOPK_RANK):
            m = _column_reduce(cands, jnp.maximum, jnp.max)
            hits = [c == m for c in cands]
            cnt = _column_reduce([jnp.where(e, 1.0, 0.0) for e in hits], jnp.add, jnp.sum)
            cands = [jnp.where(e, NEG_INF, c) for e, c in zip(hits, cands)]
            top = m if top is None else top
            take = jnp.minimum(cnt, jnp.maximum(PEER_TOPK - cum, 0.0))
            z = z + take * jnp.exp(m - top)
            reached = cum + cnt
            v_in = jnp.where((cum < PEER_TOPK) & (reached >= PEER_TOPK), m, v_in)
            v_out = jnp.where((cum < TOPK_RANK) & (reached >= TOPK_RANK), m, v_out)
            cum = reached
        s1_ref[h] = s1
        s2_ref[h] = s2
        e1_ref[h] = jnp.exp(s1 - a_scr[0:1, :])
        e2_ref[h] = jnp.exp(s2 - b_scr[0:1, :]) / z
        thr_ref[h] = 0.5 * (v_in + v_out)


def _topk(qp, keys_bf, tb):
    t = qp.shape[0]
    big = pl.BlockSpec((PEER_HEADS, PEER_NKEYS, tb), lambda i: (0, 0, i))
    big_sds = jax.ShapeDtypeStruct((PEER_HEADS, PEER_NKEYS, t), F32)
    return pl.pallas_call(
        _topk_kernel,
        grid=(t // tb,),
        in_specs=[pl.BlockSpec((tb, D_MODEL), lambda i: (i, 0)),
                  pl.BlockSpec((PEER_HEADS, 2, PEER_NKEYS, PEER_DK), lambda i: (0, 0, 0, 0))],
        out_specs=[big, big, big, big, pl.BlockSpec((PEER_HEADS, 1, tb), lambda i: (0, 0, i))],
        out_shape=[big_sds, big_sds, big_sds, big_sds, jax.ShapeDtypeStruct((PEER_HEADS, 1, t), F32)],
        scratch_shapes=[pltpu.VMEM((TOPK_ROWS, tb), F32), pltpu.VMEM((TOPK_ROWS, tb), F32)],
        compiler_params=_params(("arbitrary",), 32),
        name="peer_topk",
    )(qp, keys_bf)


_SQRT_HALF = float(np.sqrt(0.5))


def _gelu(x):
    return 0.5 * x * (1.0 + lax.erf(x * _SQRT_HALF))


MXU_TILE = 256
HID_GROUPS = (512, 512)


def _peer_kernel(ht_ref, u_ref, vt_ref, s1_ref, s2_ref, e1_ref, e2_ref, thr_ref, o_ref, acc_scr, *w_refs):
    j = pl.program_id(1)
    tb = ht_ref.shape[1]
    ec = u_ref.shape[0]
    assert ec == SUBLANES * PEER_NKEYS
    n_k, n_n = ec // MXU_TILE, tb // MXU_TILE
    assert len(w_refs) == n_k * n_n

    @pl.when(j == 0)
    def _():
        acc_scr[...] = jnp.zeros_like(acc_scr)

    row0 = pl.multiple_of(j * SUBLANES, SUBLANES)
    assert sum(HID_GROUPS) == ec
    starts = [sum(HID_GROUPS[:g]) for g in range(len(HID_GROUPS))]
    hids = [jnp.dot(u_ref[s:s + n, :], ht_ref[...], preferred_element_type=F32)
            for s, n in zip(starts, HID_GROUPS)]
    per_tile = MXU_TILE // PEER_NKEYS
    for k in range(n_k):
        for n in range(n_n):
            w_ref = w_refs[k * n_n + n]
            for a in range(per_tile):
                ii = k * per_tile + a
                q = max(g for g, s in enumerate(starts) if s <= ii * PEER_NKEYS)
                r = ii * PEER_NKEYS - starts[q]
                for b in range(MXU_TILE // LANES):
                    lc = n * (MXU_TILE // LANES) + b
                    sl = slice(lc * LANES, (lc + 1) * LANES)
                    gate = jnp.zeros((PEER_NKEYS, LANES), F32)
                    for h in range(PEER_HEADS):
                        s1_rows = s1_ref[h, pl.ds(row0, SUBLANES), sl]
                        e1_rows = e1_ref[h, pl.ds(row0, SUBLANES), sl]
                        bound = thr_ref[h, :, sl] - s1_rows[ii:ii + 1, :]
                        val = e2_ref[h, :, sl] * e1_rows[ii:ii + 1, :]
                        gate = gate + jnp.where(s2_ref[h, :, sl] >= bound, val, 0.0)
                    hid = hids[q][r:r + PEER_NKEYS, sl]
                    w_ref[a * PEER_NKEYS:(a + 1) * PEER_NKEYS, b * LANES:(b + 1) * LANES] = (
                        gate * _gelu(hid)).astype(BF16)
    w_all = jnp.concatenate([jnp.concatenate([w_refs[k * n_n + n][...] for n in range(n_n)], axis=1)
                             for k in range(n_k)], axis=0)
    acc_scr[...] += jnp.dot(vt_ref[...], w_all, preferred_element_type=F32)

    @pl.when(j == pl.num_programs(1) - 1)
    def _():
        o_ref[...] = acc_scr[...].T


def _peer(h2t, u_bf, vt_bf, s1, s2, e1, e2, thr, tb, ec):
    t = h2t.shape[1]
    n_chunks = u_bf.shape[0] // ec
    big = pl.BlockSpec((PEER_HEADS, PEER_NKEYS, tb), lambda i, j: (0, 0, i))
    n_tiles = (ec // MXU_TILE) * (tb // MXU_TILE)
    return pl.pallas_call(
        _peer_kernel,
        grid=(t // tb, n_chunks),
        in_specs=[pl.BlockSpec((D_MODEL, tb), lambda i, j: (0, i)),
                  pl.BlockSpec((ec, D_MODEL), lambda i, j: (j, 0)),
                  pl.BlockSpec((D_MODEL, ec), lambda i, j: (0, j)),
                  big, big, big, big,
                  pl.BlockSpec((PEER_HEADS, 1, tb), lambda i, j: (0, 0, i))],
        out_specs=pl.BlockSpec((tb, D_MODEL), lambda i, j: (i, 0)),
        out_shape=jax.ShapeDtypeStruct((t, D_MODEL), F32),
        scratch_shapes=[pltpu.VMEM((D_MODEL, tb), F32)] + [pltpu.VMEM((MXU_TILE, MXU_TILE), BF16)] * n_tiles,
        compiler_params=_params(("arbitrary", "arbitrary"), 56),
        name="peer",
    )(h2t, u_bf, vt_bf, s1, s2, e1, e2, thr)


def _final_kernel(x1_ref, p_ref, g2_ref, fg_ref, y_ref):
    x2 = x1_ref[...] + (1.0 + _rows(g2_ref)) * p_ref[...]
    y_ref[...] = x2 * lax.rsqrt(jnp.mean(x2 * x2, axis=-1, keepdims=True) + EPS) * fg_ref[...]


def _final(x1, peer_out, mods, group_rows, final_g, tm):
    t = x1.shape[0]
    tile = pl.BlockSpec((tm, D_MODEL), lambda i: (i, 0))
    return pl.pallas_call(
        _final_kernel,
        grid=(t // tm,),
        in_specs=[tile, tile, _mod_spec(mods, 5, tm, group_rows, 1),
                  pl.BlockSpec((1, D_MODEL), lambda i: (0, 0))],
        out_specs=tile,
        out_shape=jax.ShapeDtypeStruct((t, D_MODEL), F32),
        compiler_params=_params(("arbitrary",), 48),
        name="final",
    )(x1, peer_out, mods, final_g.reshape(1, D_MODEL))


PAST_LEN = 2048
MIXIN_TOKENS = 512
TOKEN_TILE = 256
PEER_TOKENS = 512
PEER_CHUNK = 1024
TOPK_TOKENS = 256


def _channel_tail(x, conv_raw, attn, mods, group_rows, weights):
    x1, h2t, qp = _finish(x, conv_raw, attn, mods, group_rows, weights["ln_g"], weights["ln_b"],
                          weights["n2_g"], weights["wout"], weights["wq"], TOKEN_TILE)
    s1, s2, e1, e2, thr = _topk(qp, weights["keys"], TOPK_TOKENS)
    peer_out = _peer(h2t, weights["u"], weights["vt"], s1, s2, e1, e2, thr, PEER_TOKENS, PEER_CHUNK)
    return _final(x1, peer_out, mods, group_rows, weights["final_g"], TOKEN_TILE)


def kernel(x_prompt, x_sample, cache_k, cache_v, state_conv, c_prompt, c_sample, w_ada, b_ada, norm1_g, w_in,
           conv_w, conv_b, conv_ln_g, conv_ln_b, attn_out_g, w_out, norm2_g, peer_wq, peer_keys, peer_u, peer_v,
           final_g):
    b, seq, d = x_prompt.shape
    db, n_new, _ = x_sample.shape
    depth = w_ada.shape[0]
    assert depth == 1 and d == D_MODEL
    cache_len = cache_k.shape[2]
    n_sample = db * n_new
    assert seq % MIXIN_TOKENS == 0 and n_sample % PEER_TOKENS == 0

    c_rows = jnp.concatenate([c_prompt, jnp.repeat(c_sample, n_new, axis=0)], axis=0)
    mods = _ada(c_rows, w_ada[0], b_ada[0])
    mods_p = mods[:b].reshape(b, 1, N_ADA * D_MODEL)
    mods_s = mods[b:]

    weights = {
        "ln_g": conv_ln_g[0], "ln_b": conv_ln_b[0], "n2_g": norm2_g[0],
        "wout": w_out[0].astype(BF16), "wq": peer_wq[0].astype(BF16),
        "keys": peer_keys[0].astype(BF16), "u": peer_u[0].astype(BF16),
        "vt": peer_v[0].T.astype(BF16), "final_g": final_g,
    }
    w_in_bf = w_in[0].astype(BF16)

    xp = x_prompt.reshape(b * seq, d)
    tables_p = _rope_tables(np.arange(seq))
    glu, q, k, v = _mixin(xp, mods_p, seq, norm1_g[0], w_in_bf, tables_p, MIXIN_TOKENS)
    glu3 = glu.reshape(b, seq, CONV_CH)
    conv_raw = _conv_prompt(glu3, jnp.zeros((b, CONV_PAD, CONV_CH), F32), conv_w[0], conv_b[0])
    attn = _attn_prompt(q.reshape(b, seq, ATT_W), k.reshape(b, seq, ATT_W), v.reshape(b, seq, ATT_W),
                        attn_out_g[0])
    y_prompt = _channel_tail(xp, conv_raw.reshape(b * seq, CONV_CH), attn.reshape(b * seq, ATT_W),
                             mods_p, seq, weights).reshape(b, seq, d)
    keep = min(WINDOWS[2], seq)
    new_k_prompt = k.reshape(b, seq, N_HEADS, HEAD_DIM)[:, seq - keep:][None]
    new_v_prompt = v.reshape(b, seq, N_HEADS, HEAD_DIM)[:, seq - keep:][None]
    new_conv_prompt = glu3[:, seq - CONV_HIST:][None]

    xs = x_sample.reshape(n_sample, d)
    tables_s = tuple(jnp.tile(t, (TOKEN_TILE // n_new, 1)) for t in _rope_tables(PAST_LEN + np.arange(n_new)))
    glu_s, q_s, k_s, v_s = _mixin(xs, mods_s, None, norm1_g[0], w_in_bf, tables_s, TOKEN_TILE)
    glu_s3 = glu_s.reshape(db, n_new, CONV_CH)
    conv_raw_s = _conv_sample(state_conv[0], glu_s3, conv_w[0], conv_b[0])
    pad8 = lambda a: jnp.pad(a.reshape(db, n_new, ATT_W), ((0, 0), (0, S_PAD - n_new), (0, 0)))
    attn_s = _attn_sample(pad8(q_s), pad8(k_s), pad8(v_s), cache_k[0], cache_v[0], attn_out_g[0], n_new,
                          cache_len)
    attn_s = attn_s[:, :n_new].reshape(n_sample, ATT_W)
    y_sample = _channel_tail(xs, conv_raw_s.reshape(n_sample, CONV_CH), attn_s,
                             mods_s, None, weights).reshape(db, n_new, d)
    new_k_sample = k_s.reshape(db, n_new, N_HEADS, HEAD_DIM)[None]
    new_v_sample = v_s.reshape(db, n_new, N_HEADS, HEAD_DIM)[None]
    new_conv_sample = jnp.concatenate([state_conv[0], glu_s3], axis=1)[:, n_new:][None]

    return (y_prompt, y_sample, new_k_prompt, new_v_prompt, new_conv_prompt,
            new_k_sample, new_v_sample, new_conv_sample)
```

```python
import functools

import numpy as np
import jax
import jax.numpy as jnp
from jax import lax
from jax.experimental import pallas as pl
from jax.experimental.pallas import tpu as pltpu

F32 = jnp.float32
BF16 = jnp.bfloat16

D_MODEL = 2048
CONV_CH = 1024
CONV_W = 31
N_HEADS = 8
HEAD_DIM = 128
ATT_W = N_HEADS * HEAD_DIM
ROT_DIM = HEAD_DIM // 4
ROPE_THETA = 500000.0
WINDOWS = (128, 512, 2048)
DILATIONS = (1, 4, 16)
Q_BLOCK = 128
ATT_SCALE = HEAD_DIM ** -0.5
N_ADA = 6
PEER_HEADS = 8
PEER_NKEYS = 128
PEER_DK = 128
PEER_TOPK = 16
EPS = 1e-6

LANES = 128
SUBLANES = 8
MIB = 1024 * 1024

NEG_INF = float("-inf")


def _nt_dot(a, b):
    return lax.dot_general(a, b, (((1,), (1,)), ((), ())), preferred_element_type=F32)


def _rows(ref):
    return ref[0] if len(ref.shape) == 3 else ref[...]


def _params(sem, vmem_mib):
    return pltpu.CompilerParams(dimension_semantics=sem, vmem_limit_bytes=vmem_mib * MIB)


def _ada_kernel(c_ref, w_ref, b_ref, o_ref):
    c = c_ref[...]
    a = (c * jax.nn.sigmoid(c)).astype(BF16)
    o_ref[...] = jnp.dot(a, w_ref[...].astype(BF16), preferred_element_type=F32) + b_ref[...]


def _ada(c_rows, w_ada, b_ada):
    rows = c_rows.shape[0]
    n = w_ada.shape[1]
    tn = 1024
    return pl.pallas_call(
        _ada_kernel,
        grid=(n // tn,),
        in_specs=[pl.BlockSpec((rows, D_MODEL), lambda j: (0, 0)),
                  pl.BlockSpec((D_MODEL, tn), lambda j: (0, j)),
                  pl.BlockSpec((1, tn), lambda j: (0, j))],
        out_specs=pl.BlockSpec((rows, tn), lambda j: (0, j)),
        out_shape=jax.ShapeDtypeStruct((rows, n), F32),
        compiler_params=_params(("arbitrary",), 48),
        name="ada",
    )(c_rows, w_ada, b_ada.reshape(1, n))


def _rope_heads(z, cos, sin_lo, sin_hi):
    outs = []
    for h in range(N_HEADS):
        zh = z[:, h * HEAD_DIM:(h + 1) * HEAD_DIM]
        up = pltpu.roll(zh, HEAD_DIM - ROT_DIM // 2, 1)
        dn = pltpu.roll(zh, ROT_DIM // 2, 1)
        outs.append(zh * cos + up * sin_lo + dn * sin_hi)
    return jnp.concatenate(outs, axis=1)


def _mixin_kernel(x_ref, sh_ref, sc_ref, g_ref, w_ref, cos_ref, slo_ref, shi_ref,
                  glu_ref, q_ref, k_ref, v_ref):
    x = x_ref[...]
    y = x * lax.rsqrt(jnp.mean(x * x, axis=-1, keepdims=True) + EPS)
    h = ((y * g_ref[...]) * (1.0 + _rows(sc_ref)) + _rows(sh_ref)).astype(BF16)

    def group(c):
        return jnp.dot(h, w_ref[:, c * ATT_W:(c + 1) * ATT_W], preferred_element_type=F32)

    glu_ref[...] = group(0) * jax.nn.sigmoid(group(1))
    q_ref[...] = _rope_heads(group(2), cos_ref[...], slo_ref[...], shi_ref[...])
    k_ref[...] = _rope_heads(group(3), cos_ref[...], slo_ref[...], shi_ref[...])
    v_ref[...] = group(4)


def _rope_tables(pos):
    inv = ROPE_THETA ** (-np.arange(0, ROT_DIM, 2, dtype=np.float32) / ROT_DIM)
    ang = (pos.astype(np.float32)[:, None] * inv[None, :]).astype(np.float32)
    cos, sin = np.cos(ang), np.sin(ang)
    half = ROT_DIM // 2
    n = pos.shape[0]
    c = np.ones((n, HEAD_DIM), np.float32)
    c[:, :half] = cos
    c[:, half:ROT_DIM] = cos
    s_lo = np.zeros((n, HEAD_DIM), np.float32)
    s_lo[:, :half] = -sin
    s_hi = np.zeros((n, HEAD_DIM), np.float32)
    s_hi[:, half:ROT_DIM] = sin
    return jnp.asarray(c), jnp.asarray(s_lo), jnp.asarray(s_hi)


def _mod_spec(mods, col, tm, group_rows, ngrid):
    if mods.ndim == 3:
        assert group_rows % tm == 0
        per_group = group_rows // tm
        if ngrid == 1:
            return pl.BlockSpec((1, 1, D_MODEL), lambda i: (i // per_group, 0, col))
        return pl.BlockSpec((1, 1, D_MODEL), lambda i, j: (i // per_group, 0, col))
    if ngrid == 1:
        return pl.BlockSpec((tm, D_MODEL), lambda i: (i, col))
    return pl.BlockSpec((tm, D_MODEL), lambda i, j: (i, col))


def _mixin(x, mods, group_rows, norm_g, w_in_bf, tables, tm):
    t = x.shape[0]
    cos, s_lo, s_hi = tables
    pos_blocks = cos.shape[0] // tm
    assert w_in_bf.shape[1] == 5 * ATT_W and CONV_CH == ATT_W
    tab_spec = pl.BlockSpec((tm, HEAD_DIM), lambda i: (i % pos_blocks, 0))
    out_spec = pl.BlockSpec((tm, ATT_W), lambda i: (i, 0))
    out_sds = jax.ShapeDtypeStruct((t, ATT_W), F32)
    return pl.pallas_call(
        _mixin_kernel,
        grid=(t // tm,),
        in_specs=[pl.BlockSpec((tm, D_MODEL), lambda i: (i, 0)),
                  _mod_spec(mods, 0, tm, group_rows, 1),
                  _mod_spec(mods, 1, tm, group_rows, 1),
                  pl.BlockSpec((1, D_MODEL), lambda i: (0, 0)),
                  pl.BlockSpec(w_in_bf.shape, lambda i: (0, 0), pipeline_mode=pl.Buffered(1)),
                  tab_spec, tab_spec, tab_spec],
        out_specs=[out_spec, out_spec, out_spec, out_spec],
        out_shape=[out_sds, out_sds, out_sds, out_sds],
        compiler_params=_params(("arbitrary",), 56),
        name="mixin",
    )(x, mods, mods, norm_g.reshape(1, D_MODEL), w_in_bf, cos, s_lo, s_hi)


CONV_HIST = CONV_W - 1
CONV_PAD = 32
CONV_ROWS = 64


def _conv_prompt_kernel(g_ref, hist_ref, w_ref, b_ref, o_ref, xp_scr):
    seq = g_ref.shape[1]
    xp_scr[0:CONV_PAD, :] = hist_ref[0]
    xp_scr[CONV_PAD:CONV_PAD + seq, :] = g_ref[0]
    lead = CONV_PAD - CONV_HIST

    def body(c, carry):
        t0 = pl.multiple_of(c * CONV_ROWS, CONV_ROWS)
        win = xp_scr[pl.ds(t0, CONV_ROWS + CONV_PAD), :]
        acc = jnp.zeros((CONV_ROWS, LANES), F32) + b_ref[...]
        for sub in range(SUBLANES):
            shifted = win if sub == 0 else pltpu.roll(win, CONV_ROWS + CONV_PAD - sub, 0)
            for a in range(CONV_PAD // SUBLANES + 1):
                off = a * SUBLANES + sub
                j = off - lead
                if 0 <= j < CONV_W:
                    acc = acc + shifted[a * SUBLANES:a * SUBLANES + CONV_ROWS, :] * w_ref[j:j + 1, :]
        o_ref[0, pl.ds(t0, CONV_ROWS), :] = acc
        return carry

    lax.fori_loop(0, seq // CONV_ROWS, body, 0)


def _conv_prompt(glu3, hist_pad, conv_w, conv_b):
    b, seq, ch = glu3.shape
    return pl.pallas_call(
        _conv_prompt_kernel,
        grid=(b, ch // LANES),
        in_specs=[pl.BlockSpec((1, seq, LANES), lambda i, c: (i, 0, c)),
                  pl.BlockSpec((1, CONV_PAD, LANES), lambda i, c: (i, 0, c)),
                  pl.BlockSpec((CONV_W, LANES), lambda i, c: (0, c)),
                  pl.BlockSpec((1, LANES), lambda i, c: (0, c))],
        out_specs=pl.BlockSpec((1, seq, LANES), lambda i, c: (i, 0, c)),
        out_shape=jax.ShapeDtypeStruct((b, seq, ch), F32),
        scratch_shapes=[pltpu.VMEM((seq + CONV_PAD, LANES), F32)],
        compiler_params=_params(("arbitrary", "arbitrary"), 32),
        name="conv_prompt",
    )(glu3, hist_pad, conv_w, conv_b.reshape(1, ch))


def _conv_sample_kernel(hist_ref, g_ref, wh_ref, wn_ref, b_ref, o_ref):
    hist = hist_ref[...]
    g = g_ref[...]
    n_new = g.shape[1]
    for t in range(n_new):
        y = jnp.sum(hist * wh_ref[t][None], axis=1) + jnp.sum(g * wn_ref[t][None], axis=1)
        o_ref[t] = y + b_ref[...]


def _conv_sample(state, glu3, conv_w, conv_b):
    b, n_new, ch = glu3.shape
    wh = jnp.stack([jnp.pad(conv_w[:CONV_HIST - t], ((t, 0), (0, 0))) for t in range(n_new)])
    wn = jnp.stack([jnp.pad(conv_w[CONV_HIST - t:], ((0, n_new - 1 - t), (0, 0))) for t in range(n_new)])
    nb = 32
    return pl.pallas_call(
        _conv_sample_kernel,
        grid=(b // nb, ch // LANES),
        in_specs=[pl.BlockSpec((nb, CONV_HIST, LANES), lambda i, c: (i, 0, c)),
                  pl.BlockSpec((nb, n_new, LANES), lambda i, c: (i, 0, c)),
                  pl.BlockSpec((n_new, CONV_HIST, LANES), lambda i, c: (0, 0, c)),
                  pl.BlockSpec((n_new, n_new, LANES), lambda i, c: (0, 0, c)),
                  pl.BlockSpec((1, LANES), lambda i, c: (0, c))],
        out_specs=pl.BlockSpec((n_new, nb, LANES), lambda i, c: (0, i, c)),
        out_shape=jax.ShapeDtypeStruct((n_new, b, ch), F32),
        compiler_params=_params(("arbitrary", "arbitrary"), 32),
        name="conv_sample",
    )(state, glu3, wh, wn, conv_b.reshape(1, ch)).transpose(1, 0, 2)


ATTN_UNROLL = 16
MERGE_ROWS = 256


def _head_norm(o, g):
    return o * lax.rsqrt(jnp.mean(o * o, axis=-1, keepdims=True) + EPS) * g


def _attn_prompt_kernel(q_ref, k_ref, v_ref, g_ref, o_ref, acc_scr, m_scr, l_scr):
    seq = q_ref.shape[1]
    qi = lax.broadcasted_iota(jnp.int32, (Q_BLOCK, 2 * Q_BLOCK), 0)
    kj2 = lax.broadcasted_iota(jnp.int32, (Q_BLOCK, 2 * Q_BLOCK), 1)

    for pat, (win, dil) in enumerate(zip(WINDOWS, DILATIONS)):
        assert win // dil == Q_BLOCK
        n_blk = seq // dil // Q_BLOCK

        def rows(ref, start, dil=dil):
            if dil == 1:
                return ref[0, pl.ds(start, Q_BLOCK), :]
            return ref[0, pl.ds(start, Q_BLOCK, stride=dil), :]

        def put(ref, start, val, dil=dil, pat=pat):
            if dil == 1:
                ref[pat, pl.ds(start, Q_BLOCK), :] = val
            else:
                ref[pat, pl.ds(start, Q_BLOCK, stride=dil), :] = val

        def body(i, carry, dil=dil, n_blk=n_blk, rows=rows, put=put):
            r = i // n_blk
            n = i % n_blk
            cur = r + dil * Q_BLOCK * n
            prev = r + dil * Q_BLOCK * jnp.maximum(n - 1, 0)
            if dil == 1:
                cur = pl.multiple_of(cur, Q_BLOCK)
                prev = pl.multiple_of(prev, Q_BLOCK)
            qb = rows(q_ref, cur).astype(BF16)
            k2 = jnp.concatenate([rows(k_ref, prev), rows(k_ref, cur)], axis=0).astype(BF16)
            v2 = jnp.concatenate([rows(v_ref, prev), rows(v_ref, cur)], axis=0).astype(BF16)
            has_prev = jnp.minimum(n, 1)
            lo = qi * has_prev + Q_BLOCK * (1 - has_prev)
            s = _nt_dot(qb, k2) * ATT_SCALE
            s = jnp.where(kj2 >= lo, jnp.where(kj2 <= qi + Q_BLOCK, s, NEG_INF), NEG_INF)
            m = jnp.max(s, axis=-1, keepdims=True)
            p = jnp.exp(s - m).astype(BF16)
            v_ext = jnp.concatenate([v2, jnp.ones_like(v2)], axis=1)
            acc = jnp.dot(p, v_ext, preferred_element_type=F32)
            put(acc_scr, cur, acc[:, :HEAD_DIM])
            put(m_scr, cur, jnp.broadcast_to(m, (Q_BLOCK, HEAD_DIM)))
            put(l_scr, cur, acc[:, HEAD_DIM:])
            return carry

        lax.fori_loop(0, dil * n_blk, body, 0, unroll=ATTN_UNROLL)

    n_pat = len(WINDOWS)

    def merge(c, carry):
        sl = pl.ds(pl.multiple_of(c * MERGE_ROWS, MERGE_ROWS), MERGE_ROWS)
        ms = [m_scr[p, sl, :] for p in range(n_pat)]
        top = functools.reduce(jnp.maximum, ms)
        ws = [jnp.exp(m - top) for m in ms]
        num = functools.reduce(jnp.add, [w * acc_scr[p, sl, :] for p, w in enumerate(ws)])
        den = functools.reduce(jnp.add, [w * l_scr[p, sl, :] for p, w in enumerate(ws)])
        o_ref[0, sl, :] = _head_norm(num / den, g_ref[0])
        return carry

    lax.fori_loop(0, seq // MERGE_ROWS, merge, 0)


def _attn_prompt(q3, k3, v3, head_g):
    b, seq, _ = q3.shape
    spec = pl.BlockSpec((1, seq, HEAD_DIM), lambda i, h: (i, 0, h))
    return pl.pallas_call(
        _attn_prompt_kernel,
        grid=(b, N_HEADS),
        in_specs=[spec, spec, spec, pl.BlockSpec((1, 1, HEAD_DIM), lambda i, h: (h, 0, 0))],
        out_specs=spec,
        out_shape=jax.ShapeDtypeStruct((b, seq, ATT_W), F32),
        scratch_shapes=[pltpu.VMEM((len(WINDOWS), seq, HEAD_DIM), F32)] * 3,
        compiler_params=_params(("arbitrary", "arbitrary"), 32),
        name="attn_prompt",
    )(q3, k3, v3, head_g.reshape(N_HEADS, 1, HEAD_DIM))


S_PAD = 8
TAIL = 512
N_COLS = LANES


def _attn_sample_kernel(q_ref, kn_ref, vn_ref, kt_ref, vt_ref, kd_ref, vd_ref, g_ref, o_ref,
                        k_scr, v_scr, *, n_new, past):
    n_dil = past // DILATIONS[2]
    kd = jnp.swapaxes(kd_ref[0], 0, 1)
    vd = jnp.swapaxes(vd_ref[0], 0, 1)
    kt = jnp.swapaxes(kt_ref[0].reshape(TAIL, N_HEADS, HEAD_DIM), 0, 1)
    vt = jnp.swapaxes(vt_ref[0].reshape(TAIL, N_HEADS, HEAD_DIM), 0, 1)
    for h in range(N_HEADS):
        lanes = slice(h * HEAD_DIM, (h + 1) * HEAD_DIM)
        k_scr[0:TAIL, lanes] = kt[h].astype(BF16)
        v_scr[0:TAIL, lanes] = vt[h].astype(BF16)
        for res in range(n_new):
            dst = slice(TAIL + res * n_dil, TAIL + (res + 1) * n_dil)
            k_scr[dst, lanes] = kd[res * N_HEADS + h].astype(BF16)
            v_scr[dst, lanes] = vd[res * N_HEADS + h].astype(BF16)
    dil0 = TAIL

    q8 = q_ref[0]
    qt = jnp.concatenate([q8] * N_HEADS + [jnp.zeros_like(q8)] * (N_COLS // S_PAD - N_HEADS), axis=0)
    row = lax.broadcasted_iota(jnp.int32, (N_COLS, ATT_W), 0)
    lane = lax.broadcasted_iota(jnp.int32, (N_COLS, ATT_W), 1)
    qbd = jnp.where(row // S_PAD == lane // HEAD_DIM, qt, 0.0).astype(BF16)

    def col_query(shape):
        return lax.broadcasted_iota(jnp.int32, shape, 1) % S_PAD % n_new

    def key_row(shape):
        return lax.broadcasted_iota(jnp.int32, shape, 0)

    groups = []
    s = _nt_dot(k_scr[0:TAIL, :], qbd) * ATT_SCALE
    r, c = key_row(s.shape), col_query(s.shape)
    base = past - TAIL
    mult = ((((base + r) % DILATIONS[1] == (past + c) % DILATIONS[1])
             & (base + r >= past + c - WINDOWS[1])).astype(F32)
            + (base + r >= past + c - WINDOWS[0]).astype(F32))
    groups.append((s, mult, v_scr[0:TAIL, :]))
    for res in range(n_new):
        rows = slice(dil0 + res * n_dil, dil0 + (res + 1) * n_dil)
        s = _nt_dot(k_scr[rows, :], qbd) * ATT_SCALE
        mult = (col_query(s.shape) == res).astype(F32)
        groups.append((s, mult, v_scr[rows, :]))
    s = _nt_dot(kn_ref[0].astype(BF16), qbd) * ATT_SCALE
    r, c = key_row(s.shape), col_query(s.shape)
    mult = (r <= c).astype(F32) + 2.0 * (r == c).astype(F32)
    groups.append((s, mult, vn_ref[0].astype(BF16)))

    m = None
    for s, mult, _ in groups:
        gm = jnp.max(jnp.where(mult > 0, s, NEG_INF), axis=0, keepdims=True)
        m = gm if m is None else jnp.maximum(m, gm)
    num = jnp.zeros((N_COLS, ATT_W), F32)
    den = jnp.zeros((N_COLS, LANES), F32)
    for s, mult, v in groups:
        p = mult * jnp.exp(jnp.where(mult > 0, s - m, NEG_INF))
        pt = p.T.astype(BF16)
        num = num + jnp.dot(pt, v, preferred_element_type=F32)
        den = den + jnp.dot(pt, jnp.ones((v.shape[0], LANES), BF16), preferred_element_type=F32)
    outs = []
    for h in range(N_HEADS):
        o = (num[h * S_PAD:(h + 1) * S_PAD, h * HEAD_DIM:(h + 1) * HEAD_DIM]
             / den[h * S_PAD:(h + 1) * S_PAD, :])
        outs.append(_head_norm(o, g_ref[h]))
    o_ref[0] = jnp.concatenate(outs, axis=1)


def _attn_sample(q8, k8, v8, cache_k, cache_v, head_g, n_new, past):
    b = q8.shape[0]
    lw = cache_k.shape[1]
    assert lw == past and lw % 16 == 0 and lw >= WINDOWS[2] and n_new <= 4
    dil = DILATIONS[2]
    ck_rows = cache_k.reshape(b, lw * N_HEADS, HEAD_DIM)
    cv_rows = cache_v.reshape(b, lw * N_HEADS, HEAD_DIM)
    ck_grp = cache_k.reshape(b, lw // dil, dil * N_HEADS, HEAD_DIM)
    cv_grp = cache_v.reshape(b, lw // dil, dil * N_HEADS, HEAD_DIM)
    new_spec = pl.BlockSpec((1, S_PAD, ATT_W), lambda i: (i, 0, 0))
    tail_spec = pl.BlockSpec((1, TAIL * N_HEADS, HEAD_DIM), lambda i: (i, lw // TAIL - 1, 0))
    dil_spec = pl.BlockSpec((1, lw // dil, n_new * N_HEADS, HEAD_DIM), lambda i: (i, 0, 0, 0))
    n_keys = TAIL + n_new * (lw // dil)
    return pl.pallas_call(
        functools.partial(_attn_sample_kernel, n_new=n_new, past=past),
        grid=(b,),
        in_specs=[new_spec, new_spec, new_spec, tail_spec, tail_spec, dil_spec, dil_spec,
                  pl.BlockSpec((N_HEADS, 1, HEAD_DIM), lambda i: (0, 0, 0))],
        out_specs=new_spec,
        out_shape=jax.ShapeDtypeStruct((b, S_PAD, ATT_W), F32),
        scratch_shapes=[pltpu.VMEM((n_keys, ATT_W), BF16), pltpu.VMEM((n_keys, ATT_W), BF16)],
        compiler_params=_params(("arbitrary",), 48),
        name="attn_sample",
    )(q8, k8, v8, ck_rows, cv_rows, ck_grp, cv_grp, head_g.reshape(N_HEADS, 1, HEAD_DIM))


def _finish_kernel(x_ref, conv_ref, attn_ref, g1_ref, sh2_ref, sc2_ref, lng_ref, lnb_ref, n2g_ref,
                   wout_ref, wq_ref, x1_ref, h2t_ref, qp_ref, cat_scr):
    y = conv_ref[...]
    mu = jnp.mean(y, axis=-1, keepdims=True)
    yc = y - mu
    var = jnp.mean(yc * yc, axis=-1, keepdims=True)
    z = yc * lax.rsqrt(var + EPS) * lng_ref[...] + lnb_ref[...]
    cat_scr[:, :CONV_CH] = (z * jax.nn.sigmoid(z)).astype(BF16)
    cat_scr[:, CONV_CH:] = attn_ref[...].astype(BF16)
    mix = jnp.dot(cat_scr[...], wout_ref[...], preferred_element_type=F32)
    x1 = x_ref[...] + (1.0 + _rows(g1_ref)) * mix
    x1_ref[...] = x1
    n = x1 * lax.rsqrt(jnp.mean(x1 * x1, axis=-1, keepdims=True) + EPS)
    h2 = (n * n2g_ref[...]) * (1.0 + _rows(sc2_ref)) + _rows(sh2_ref)
    h2t_ref[...] = h2.T.astype(BF16)
    qp_ref[...] = jnp.dot(h2.astype(BF16), wq_ref[...], preferred_element_type=F32).astype(BF16)


def _finish(x, conv_raw, attn, mods, group_rows, ln_g, ln_b, n2_g, wout_bf, wq_bf, tm):
    t = x.shape[0]
    row = lambda n: pl.BlockSpec((1, n), lambda i: (0, 0))
    const = lambda shape: pl.BlockSpec(shape, lambda i: (0, 0), pipeline_mode=pl.Buffered(1))
    tile = lambda n: pl.BlockSpec((tm, n), lambda i: (i, 0))
    return pl.pallas_call(
        _finish_kernel,
        grid=(t // tm,),
        in_specs=[tile(D_MODEL), tile(CONV_CH), tile(ATT_W),
                  _mod_spec(mods, 2, tm, group_rows, 1),
                  _mod_spec(mods, 3, tm, group_rows, 1),
                  _mod_spec(mods, 4, tm, group_rows, 1),
                  row(CONV_CH), row(CONV_CH), row(D_MODEL),
                  const((D_MODEL, D_MODEL)), const((D_MODEL, D_MODEL))],
        out_specs=[tile(D_MODEL), pl.BlockSpec((D_MODEL, tm), lambda i: (0, i)), tile(D_MODEL)],
        out_shape=[jax.ShapeDtypeStruct((t, D_MODEL), F32),
                   jax.ShapeDtypeStruct((D_MODEL, t), BF16),
                   jax.ShapeDtypeStruct((t, D_MODEL), BF16)],
        scratch_shapes=[pltpu.VMEM((tm, D_MODEL), BF16)],
        compiler_params=_params(("arbitrary",), 56),
        name="finish",
    )(x, conv_raw, attn, mods, mods, mods, ln_g.reshape(1, CONV_CH), ln_b.reshape(1, CONV_CH),
      n2_g.reshape(1, D_MODEL), wout_bf, wq_bf)


TOPK_RANK = PEER_TOPK + 1
TOPK_ROWS = -(-TOPK_RANK // SUBLANES) * SUBLANES


def _all_sublanes(x, combine):
    shift = SUBLANES // 2
    while shift:
        x = combine(x, pltpu.roll(x, shift, 0))
        shift //= 2
    return x


def _sorting_network(n):
    pairs = []

    def merge(lo, length, r):
        step = 2 * r
        if step < length:
            merge(lo, length, step)
            merge(lo + r, length, step)
            pairs.extend((i, i + r) for i in range(lo + r, lo + length - r, step))
        else:
            pairs.append((lo, lo + r))

    def sort(lo, length):
        if length > 1:
            sort(lo, length // 2)
            sort(lo + length // 2, length // 2)
            merge(lo, length, 1)

    sort(0, n)
    return pairs


def _top_values(s, k, out_scr):
    n_tiles = s.shape[0] // SUBLANES
    v = [s[r * SUBLANES:(r + 1) * SUBLANES, :] for r in range(n_tiles)]
    for i, j in _sorting_network(n_tiles):
        v[i], v[j] = jnp.maximum(v[i], v[j]), jnp.minimum(v[i], v[j])
    out_scr[...] = jnp.full(out_scr.shape, NEG_INF, F32)
    depth = min(n_tiles, k)
    v = v[:depth]
    for i in range(k):
        m = _all_sublanes(v[0], jnp.maximum)
        out_scr[i:i + 1, :] = m[0:1, :]
        pop = v[0] == m
        live = min(depth, k - i)
        for r in range(live):
            below = v[r + 1] if r + 1 < depth else NEG_INF
            v[r] = jnp.where(pop, below, v[r])


def _pair_candidates(a_scr, b_scr):
    tb = a_scr.shape[1]
    rows_all = lax.broadcasted_iota(jnp.int32, (TOPK_ROWS, tb), 0)
    rows_one = lax.broadcasted_iota(jnp.int32, (SUBLANES, tb), 0)
    cands = [a_scr[0:1, :] + b_scr[...],
             jnp.where(rows_all >= 1, b_scr[0:1, :] + a_scr[...], NEG_INF)]
    single = []
    for i in range(1, TOPK_RANK):
        j_max = TOPK_RANK // (i + 1) - 1
        if j_max >= 2:
            assert j_max < SUBLANES
            ok = (rows_one >= 1) & (rows_one <= j_max)
            cands.append(jnp.where(ok, a_scr[i:i + 1, :] + b_scr[0:SUBLANES, :], NEG_INF))
        elif j_max == 1:
            single.append(i)
    if single:
        assert single == list(range(single[0], single[-1] + 1)) and single[-1] < SUBLANES
        ok = (rows_one >= single[0]) & (rows_one <= single[-1])
        cands.append(jnp.where(ok, b_scr[1:2, :] + a_scr[0:SUBLANES, :], NEG_INF))
    return [c[r:r + SUBLANES, :] for c in cands for r in range(0, c.shape[0], SUBLANES)]


def _column_reduce(tiles, combine):
    return _all_sublanes(functools.reduce(combine, tiles), combine)


def _topk_kernel(qp_ref, keys_ref, s1_ref, s2_ref, e1_ref, e2_ref, thr_ref, a_scr, b_scr):
    tb = qp_ref.shape[0]
    for h in range(PEER_HEADS):
        sc = []
        for p in range(2):
            col = (h * 2 + p) * PEER_DK
            sc.append(_nt_dot(keys_ref[h, p], qp_ref[:, col:col + PEER_DK]))
        s1, s2 = sc
        _top_values(s1, TOPK_RANK, a_scr)
        _top_values(s2, TOPK_RANK, b_scr)
        cands = _pair_candidates(a_scr, b_scr)
        top = None
        cum = jnp.zeros((SUBLANES, tb), F32)
        z = jnp.zeros((SUBLANES, tb), F32)
        v_in = jnp.zeros((SUBLANES, tb), F32)
        v_out = jnp.zeros((SUBLANES, tb), F32)
        for _ in range(TOPK_RANK):
            m = _column_reduce(cands, jnp.maximum)
            hits = [c == m for c in cands]
            cnt = _column_reduce([jnp.where(e, 1.0, 0.0) for e in hits], jnp.add)
            cands = [jnp.where(e, NEG_INF, c) for e, c in zip(hits, cands)]
            top = m if top is None else top
            take = jnp.minimum(cnt, jnp.maximum(PEER_TOPK - cum, 0.0))
            z = z + take * jnp.exp(m - top)
            reached = cum + cnt
            v_in = jnp.where((cum < PEER_TOPK) & (reached >= PEER_TOPK), m, v_in)
            v_out = jnp.where((cum < TOPK_RANK) & (reached >= TOPK_RANK), m, v_out)
            cum = reached
        s1_ref[h] = s1
        s2_ref[h] = s2
        e1_ref[h] = jnp.exp(s1 - a_scr[0:1, :])
        e2_ref[h] = jnp.exp(s2 - b_scr[0:1, :]) / z[0:1, :]
        thr_ref[h] = 0.5 * (v_in[0:1, :] + v_out[0:1, :])


def _topk(qp, keys_bf, tb):
    t = qp.shape[0]
    big = pl.BlockSpec((PEER_HEADS, PEER_NKEYS, tb), lambda i: (0, 0, i))
    big_sds = jax.ShapeDtypeStruct((PEER_HEADS, PEER_NKEYS, t), F32)
    return pl.pallas_call(
        _topk_kernel,
        grid=(t // tb,),
        in_specs=[pl.BlockSpec((tb, D_MODEL), lambda i: (i, 0)),
                  pl.BlockSpec((PEER_HEADS, 2, PEER_NKEYS, PEER_DK), lambda i: (0, 0, 0, 0))],
        out_specs=[big, big, big, big, pl.BlockSpec((PEER_HEADS, 1, tb), lambda i: (0, 0, i))],
        out_shape=[big_sds, big_sds, big_sds, big_sds, jax.ShapeDtypeStruct((PEER_HEADS, 1, t), F32)],
        scratch_shapes=[pltpu.VMEM((TOPK_ROWS, tb), F32), pltpu.VMEM((TOPK_ROWS, tb), F32)],
        compiler_params=_params(("arbitrary",), 32),
        name="peer_topk",
    )(qp, keys_bf)


_SQRT_HALF = float(np.sqrt(0.5))


def _gelu(x):
    return 0.5 * x * (1.0 + lax.erf(x * _SQRT_HALF))


MXU_TILE = 256
HID_GROUPS = (512, 512)


def _peer_kernel(ht_ref, u_ref, vt_ref, s1_ref, s2_ref, e1_ref, e2_ref, thr_ref, o_ref, acc_scr, *w_refs):
    j = pl.program_id(1)
    tb = ht_ref.shape[1]
    ec = u_ref.shape[0]
    assert ec == SUBLANES * PEER_NKEYS
    n_k, n_n = ec // MXU_TILE, tb // MXU_TILE
    assert len(w_refs) == n_k * n_n

    @pl.when(j == 0)
    def _():
        acc_scr[...] = jnp.zeros_like(acc_scr)

    row0 = pl.multiple_of(j * SUBLANES, SUBLANES)
    assert sum(HID_GROUPS) == ec
    starts = [sum(HID_GROUPS[:g]) for g in range(len(HID_GROUPS))]
    hids = [jnp.dot(u_ref[s:s + n, :], ht_ref[...], preferred_element_type=F32)
            for s, n in zip(starts, HID_GROUPS)]
    per_tile = MXU_TILE // PEER_NKEYS
    for k in range(n_k):
        for n in range(n_n):
            w_ref = w_refs[k * n_n + n]
            for a in range(per_tile):
                ii = k * per_tile + a
                q = max(g for g, s in enumerate(starts) if s <= ii * PEER_NKEYS)
                r = ii * PEER_NKEYS - starts[q]
                for b in range(MXU_TILE // LANES):
                    lc = n * (MXU_TILE // LANES) + b
                    sl = slice(lc * LANES, (lc + 1) * LANES)
                    gate = jnp.zeros((PEER_NKEYS, LANES), F32)
                    for h in range(PEER_HEADS):
                        s1_rows = s1_ref[h, pl.ds(row0, SUBLANES), sl]
                        e1_rows = e1_ref[h, pl.ds(row0, SUBLANES), sl]
                        bound = thr_ref[h, :, sl] - s1_rows[ii:ii + 1, :]
                        val = e2_ref[h, :, sl] * e1_rows[ii:ii + 1, :]
                        gate = gate + jnp.where(s2_ref[h, :, sl] >= bound, val, 0.0)
                    hid = hids[q][r:r + PEER_NKEYS, sl]
                    w_ref[a * PEER_NKEYS:(a + 1) * PEER_NKEYS, b * LANES:(b + 1) * LANES] = (
                        gate * _gelu(hid)).astype(BF16)
    w_all = jnp.concatenate([jnp.concatenate([w_refs[k * n_n + n][...] for n in range(n_n)], axis=1)
                             for k in range(n_k)], axis=0)
    acc_scr[...] += jnp.dot(vt_ref[...], w_all, preferred_element_type=F32)

    @pl.when(j == pl.num_programs(1) - 1)
    def _():
        o_ref[...] = acc_scr[...].T


def _peer(h2t, u_bf, vt_bf, s1, s2, e1, e2, thr, tb, ec):
    t = h2t.shape[1]
    n_chunks = u_bf.shape[0] // ec
    big = pl.BlockSpec((PEER_HEADS, PEER_NKEYS, tb), lambda i, j: (0, 0, i))
    n_tiles = (ec // MXU_TILE) * (tb // MXU_TILE)
    return pl.pallas_call(
        _peer_kernel,
        grid=(t // tb, n_chunks),
        in_specs=[pl.BlockSpec((D_MODEL, tb), lambda i, j: (0, i)),
                  pl.BlockSpec((ec, D_MODEL), lambda i, j: (j, 0)),
                  pl.BlockSpec((D_MODEL, ec), lambda i, j: (0, j)),
                  big, big, big, big,
                  pl.BlockSpec((PEER_HEADS, 1, tb), lambda i, j: (0, 0, i))],
        out_specs=pl.BlockSpec((tb, D_MODEL), lambda i, j: (i, 0)),
        out_shape=jax.ShapeDtypeStruct((t, D_MODEL), F32),
        scratch_shapes=[pltpu.VMEM((D_MODEL, tb), F32)] + [pltpu.VMEM((MXU_TILE, MXU_TILE), BF16)] * n_tiles,
        compiler_params=_params(("arbitrary", "arbitrary"), 56),
        name="peer",
    )(h2t, u_bf, vt_bf, s1, s2, e1, e2, thr)


def _final_kernel(x1_ref, p_ref, g2_ref, fg_ref, y_ref):
    x2 = x1_ref[...] + (1.0 + _rows(g2_ref)) * p_ref[...]
    y_ref[...] = x2 * lax.rsqrt(jnp.mean(x2 * x2, axis=-1, keepdims=True) + EPS) * fg_ref[...]


def _final(x1, peer_out, mods, group_rows, final_g, tm):
    t = x1.shape[0]
    tile = pl.BlockSpec((tm, D_MODEL), lambda i: (i, 0))
    return pl.pallas_call(
        _final_kernel,
        grid=(t // tm,),
        in_specs=[tile, tile, _mod_spec(mods, 5, tm, group_rows, 1),
                  pl.BlockSpec((1, D_MODEL), lambda i: (0, 0))],
        out_specs=tile,
        out_shape=jax.ShapeDtypeStruct((t, D_MODEL), F32),
        compiler_params=_params(("arbitrary",), 48),
        name="final",
    )(x1, peer_out, mods, final_g.reshape(1, D_MODEL))


PAST_LEN = 2048
MIXIN_TOKENS = 256
TOKEN_TILE = 256
PEER_TOKENS = 512
PEER_CHUNK = 1024
TOPK_TOKENS = 256


def _channel_tail(x, conv_raw, attn, mods, group_rows, weights):
    x1, h2t, qp = _finish(x, conv_raw, attn, mods, group_rows, weights["ln_g"], weights["ln_b"],
                          weights["n2_g"], weights["wout"], weights["wq"], TOKEN_TILE)
    s1, s2, e1, e2, thr = _topk(qp, weights["keys"], TOPK_TOKENS)
    peer_out = _peer(h2t, weights["u"], weights["vt"], s1, s2, e1, e2, thr, PEER_TOKENS, PEER_CHUNK)
    return _final(x1, peer_out, mods, group_rows, weights["final_g"], TOKEN_TILE)


def kernel(x_prompt, x_sample, cache_k, cache_v, state_conv, c_prompt, c_sample, w_ada, b_ada, norm1_g, w_in,
           conv_w, conv_b, conv_ln_g, conv_ln_b, attn_out_g, w_out, norm2_g, peer_wq, peer_keys, peer_u, peer_v,
           final_g):
    b, seq, d = x_prompt.shape
    db, n_new, _ = x_sample.shape
    depth = w_ada.shape[0]
    assert depth == 1 and d == D_MODEL
    cache_len = cache_k.shape[2]
    n_sample = db * n_new
    assert seq % MIXIN_TOKENS == 0 and n_sample % PEER_TOKENS == 0

    c_rows = jnp.concatenate([c_prompt, jnp.repeat(c_sample, n_new, axis=0)], axis=0)
    mods = _ada(c_rows, w_ada[0], b_ada[0])
    mods_p = mods[:b].reshape(b, 1, N_ADA * D_MODEL)
    mods_s = mods[b:]

    weights = {
        "ln_g": conv_ln_g[0], "ln_b": conv_ln_b[0], "n2_g": norm2_g[0],
        "wout": w_out[0].astype(BF16), "wq": peer_wq[0].astype(BF16),
        "keys": peer_keys[0].astype(BF16), "u": peer_u[0].astype(BF16),
        "vt": peer_v[0].T.astype(BF16), "final_g": final_g,
    }
    w_in_bf = w_in[0].astype(BF16)

    xp = x_prompt.reshape(b * seq, d)
    tables_p = _rope_tables(np.arange(seq))
    glu, q, k, v = _mixin(xp, mods_p, seq, norm1_g[0], w_in_bf, tables_p, MIXIN_TOKENS)
    glu3 = glu.reshape(b, seq, CONV_CH)
    conv_raw = _conv_prompt(glu3, jnp.zeros((b, CONV_PAD, CONV_CH), F32), conv_w[0], conv_b[0])
    attn = _attn_prompt(q.reshape(b, seq, ATT_W), k.reshape(b, seq, ATT_W), v.reshape(b, seq, ATT_W),
                        attn_out_g[0])
    y_prompt = _channel_tail(xp, conv_raw.reshape(b * seq, CONV_CH), attn.reshape(b * seq, ATT_W),
                             mods_p, seq, weights).reshape(b, seq, d)
    keep = min(WINDOWS[2], seq)
    new_k_prompt = k.reshape(b, seq, N_HEADS, HEAD_DIM)[:, seq - keep:][None]
    new_v_prompt = v.reshape(b, seq, N_HEADS, HEAD_DIM)[:, seq - keep:][None]
    new_conv_prompt = glu3[:, seq - CONV_HIST:][None]

    xs = x_sample.reshape(n_sample, d)
    tables_s = tuple(jnp.tile(t, (TOKEN_TILE // n_new, 1)) for t in _rope_tables(PAST_LEN + np.arange(n_new)))
    glu_s, q_s, k_s, v_s = _mixin(xs, mods_s, None, norm1_g[0], w_in_bf, tables_s, TOKEN_TILE)
    glu_s3 = glu_s.reshape(db, n_new, CONV_CH)
    conv_raw_s = _conv_sample(state_conv[0], glu_s3, conv_w[0], conv_b[0])
    pad8 = lambda a: jnp.pad(a.reshape(db, n_new, ATT_W), ((0, 0), (0, S_PAD - n_new), (0, 0)))
    attn_s = _attn_sample(pad8(q_s), pad8(k_s), pad8(v_s), cache_k[0], cache_v[0], attn_out_g[0], n_new,
                          cache_len)
    attn_s = attn_s[:, :n_new].reshape(n_sample, ATT_W)
    y_sample = _channel_tail(xs, conv_raw_s.reshape(n_sample, CONV_CH), attn_s,
                             mods_s, None, weights).reshape(db, n_new, d)
    new_k_sample = k_s.reshape(db, n_new, N_HEADS, HEAD_DIM)[None]
    new_v_sample = v_s.reshape(db, n_new, N_HEADS, HEAD_DIM)[None]
    new_conv_sample = jnp.concatenate([state_conv[0], glu_s3], axis=1)[:, n_new:][None]

    return (y_prompt, y_sample, new_k_prompt, new_v_prompt, new_conv_prompt,
            new_k_sample, new_v_sample, new_conv_sample)
```

```python
import functools

import numpy as np
import jax
import jax.numpy as jnp
from jax import lax
from jax.experimental import pallas as pl
from jax.experimental.pallas import tpu as pltpu

F32 = jnp.float32
BF16 = jnp.bfloat16

D_MODEL = 2048
CONV_CH = 1024
CONV_W = 31
N_HEADS = 8
HEAD_DIM = 128
ATT_W = N_HEADS * HEAD_DIM
ROT_DIM = HEAD_DIM // 4
ROPE_THETA = 500000.0
WINDOWS = (128, 512, 2048)
DILATIONS = (1, 4, 16)
Q_BLOCK = 128
ATT_SCALE = HEAD_DIM ** -0.5
N_ADA = 6
PEER_HEADS = 8
PEER_NKEYS = 128
PEER_DK = 128
PEER_TOPK = 16
EPS = 1e-6

LANES = 128
SUBLANES = 8
MIB = 1024 * 1024

NEG_INF = float("-inf")


def _nt_dot(a, b):
    return lax.dot_general(a, b, (((1,), (1,)), ((), ())), preferred_element_type=F32)


def _rows(ref):
    return ref[0] if len(ref.shape) == 3 else ref[...]


def _params(sem, vmem_mib):
    return pltpu.CompilerParams(dimension_semantics=sem, vmem_limit_bytes=vmem_mib * MIB)


def _ada_kernel(c_ref, w_ref, b_ref, o_ref):
    c = c_ref[...]
    a = (c * jax.nn.sigmoid(c)).astype(BF16)
    o_ref[...] = jnp.dot(a, w_ref[...].astype(BF16), preferred_element_type=F32) + b_ref[...]


def _ada(c_rows, w_ada, b_ada):
    rows = c_rows.shape[0]
    n = w_ada.shape[1]
    tn = 1024
    return pl.pallas_call(
        _ada_kernel,
        grid=(n // tn,),
        in_specs=[pl.BlockSpec((rows, D_MODEL), lambda j: (0, 0)),
                  pl.BlockSpec((D_MODEL, tn), lambda j: (0, j)),
                  pl.BlockSpec((1, tn), lambda j: (0, j))],
        out_specs=pl.BlockSpec((rows, tn), lambda j: (0, j)),
        out_shape=jax.ShapeDtypeStruct((rows, n), F32),
        compiler_params=_params(("arbitrary",), 48),
        name="ada",
    )(c_rows, w_ada, b_ada.reshape(1, n))


def _rope_heads(z, cos, sin_lo, sin_hi):
    outs = []
    for h in range(N_HEADS):
        zh = z[:, h * HEAD_DIM:(h + 1) * HEAD_DIM]
        up = pltpu.roll(zh, HEAD_DIM - ROT_DIM // 2, 1)
        dn = pltpu.roll(zh, ROT_DIM // 2, 1)
        outs.append(zh * cos + up * sin_lo + dn * sin_hi)
    return jnp.concatenate(outs, axis=1)


def _mixin_kernel(x_ref, sh_ref, sc_ref, g_ref, w_ref, cos_ref, slo_ref, shi_ref,
                  glu_ref, q_ref, k_ref, v_ref):
    x = x_ref[...]
    y = x * lax.rsqrt(jnp.mean(x * x, axis=-1, keepdims=True) + EPS)
    h = ((y * g_ref[...]) * (1.0 + _rows(sc_ref)) + _rows(sh_ref)).astype(BF16)

    def group(c):
        return jnp.dot(h, w_ref[:, c * ATT_W:(c + 1) * ATT_W], preferred_element_type=F32)

    glu_ref[...] = group(0) * jax.nn.sigmoid(group(1))
    q_ref[...] = _rope_heads(group(2), cos_ref[...], slo_ref[...], shi_ref[...])
    k_ref[...] = _rope_heads(group(3), cos_ref[...], slo_ref[...], shi_ref[...])
    v_ref[...] = group(4)


def _rope_tables(pos):
    inv = ROPE_THETA ** (-np.arange(0, ROT_DIM, 2, dtype=np.float32) / ROT_DIM)
    ang = (pos.astype(np.float32)[:, None] * inv[None, :]).astype(np.float32)
    cos, sin = np.cos(ang), np.sin(ang)
    half = ROT_DIM // 2
    n = pos.shape[0]
    c = np.ones((n, HEAD_DIM), np.float32)
    c[:, :half] = cos
    c[:, half:ROT_DIM] = cos
    s_lo = np.zeros((n, HEAD_DIM), np.float32)
    s_lo[:, :half] = -sin
    s_hi = np.zeros((n, HEAD_DIM), np.float32)
    s_hi[:, half:ROT_DIM] = sin
    return jnp.asarray(c), jnp.asarray(s_lo), jnp.asarray(s_hi)


def _mod_spec(mods, col, tm, group_rows, ngrid):
    if mods.ndim == 3:
        assert group_rows % tm == 0
        per_group = group_rows // tm
        if ngrid == 1:
            return pl.BlockSpec((1, 1, D_MODEL), lambda i: (i // per_group, 0, col))
        return pl.BlockSpec((1, 1, D_MODEL), lambda i, j: (i // per_group, 0, col))
    if ngrid == 1:
        return pl.BlockSpec((tm, D_MODEL), lambda i: (i, col))
    return pl.BlockSpec((tm, D_MODEL), lambda i, j: (i, col))


def _mixin(x, mods, group_rows, norm_g, w_in_bf, tables, tm):
    t = x.shape[0]
    cos, s_lo, s_hi = tables
    pos_blocks = cos.shape[0] // tm
    assert w_in_bf.shape[1] == 5 * ATT_W and CONV_CH == ATT_W
    tab_spec = pl.BlockSpec((tm, HEAD_DIM), lambda i: (i % pos_blocks, 0))
    out_spec = pl.BlockSpec((tm, ATT_W), lambda i: (i, 0))
    out_sds = jax.ShapeDtypeStruct((t, ATT_W), F32)
    return pl.pallas_call(
        _mixin_kernel,
        grid=(t // tm,),
        in_specs=[pl.BlockSpec((tm, D_MODEL), lambda i: (i, 0)),
                  _mod_spec(mods, 0, tm, group_rows, 1),
                  _mod_spec(mods, 1, tm, group_rows, 1),
                  pl.BlockSpec((1, D_MODEL), lambda i: (0, 0)),
                  pl.BlockSpec(w_in_bf.shape, lambda i: (0, 0), pipeline_mode=pl.Buffered(1)),
                  tab_spec, tab_spec, tab_spec],
        out_specs=[out_spec, out_spec, out_spec, out_spec],
        out_shape=[out_sds, out_sds, out_sds, out_sds],
        compiler_params=_params(("arbitrary",), 56),
        name="mixin",
    )(x, mods, mods, norm_g.reshape(1, D_MODEL), w_in_bf, cos, s_lo, s_hi)


CONV_HIST = CONV_W - 1
CONV_PAD = 32
CONV_ROWS = 64
CONV_UNROLL = 2


def _conv_prompt_kernel(g_ref, hist_ref, w_ref, b_ref, o_ref, xp_scr):
    seq = g_ref.shape[1]
    xp_scr[0:CONV_PAD, :] = hist_ref[0]
    xp_scr[CONV_PAD:CONV_PAD + seq, :] = g_ref[0]
    lead = CONV_PAD - CONV_HIST

    def body(c, carry):
        t0 = pl.multiple_of(c * CONV_ROWS, CONV_ROWS)
        win = xp_scr[pl.ds(t0, CONV_ROWS + CONV_PAD), :]
        acc = jnp.zeros((CONV_ROWS, LANES), F32) + b_ref[...]
        for sub in range(SUBLANES):
            shifted = win if sub == 0 else pltpu.roll(win, CONV_ROWS + CONV_PAD - sub, 0)
            for a in range(CONV_PAD // SUBLANES + 1):
                off = a * SUBLANES + sub
                j = off - lead
                if 0 <= j < CONV_W:
                    acc = acc + shifted[a * SUBLANES:a * SUBLANES + CONV_ROWS, :] * w_ref[j:j + 1, :]
        o_ref[0, pl.ds(t0, CONV_ROWS), :] = acc
        return carry

    lax.fori_loop(0, seq // CONV_ROWS, body, 0, unroll=CONV_UNROLL)


def _conv_prompt(glu3, hist_pad, conv_w, conv_b):
    b, seq, ch = glu3.shape
    return pl.pallas_call(
        _conv_prompt_kernel,
        grid=(b, ch // LANES),
        in_specs=[pl.BlockSpec((1, seq, LANES), lambda i, c: (i, 0, c)),
                  pl.BlockSpec((1, CONV_PAD, LANES), lambda i, c: (i, 0, c)),
                  pl.BlockSpec((CONV_W, LANES), lambda i, c: (0, c)),
                  pl.BlockSpec((1, LANES), lambda i, c: (0, c))],
        out_specs=pl.BlockSpec((1, seq, LANES), lambda i, c: (i, 0, c)),
        out_shape=jax.ShapeDtypeStruct((b, seq, ch), F32),
        scratch_shapes=[pltpu.VMEM((seq + CONV_PAD, LANES), F32)],
        compiler_params=_params(("arbitrary", "arbitrary"), 32),
        name="conv_prompt",
    )(glu3, hist_pad, conv_w, conv_b.reshape(1, ch))


def _conv_sample_kernel(hist_ref, g_ref, wh_ref, wn_ref, b_ref, o_ref):
    hist = hist_ref[...]
    g = g_ref[...]
    n_new = g.shape[1]
    for t in range(n_new):
        y = jnp.sum(hist * wh_ref[t][None], axis=1) + jnp.sum(g * wn_ref[t][None], axis=1)
        o_ref[t] = y + b_ref[...]


def _conv_sample(state, glu3, conv_w, conv_b):
    b, n_new, ch = glu3.shape
    wh = jnp.stack([jnp.pad(conv_w[:CONV_HIST - t], ((t, 0), (0, 0))) for t in range(n_new)])
    wn = jnp.stack([jnp.pad(conv_w[CONV_HIST - t:], ((0, n_new - 1 - t), (0, 0))) for t in range(n_new)])
    nb = 32
    return pl.pallas_call(
        _conv_sample_kernel,
        grid=(b // nb, ch // LANES),
        in_specs=[pl.BlockSpec((nb, CONV_HIST, LANES), lambda i, c: (i, 0, c)),
                  pl.BlockSpec((nb, n_new, LANES), lambda i, c: (i, 0, c)),
                  pl.BlockSpec((n_new, CONV_HIST, LANES), lambda i, c: (0, 0, c)),
                  pl.BlockSpec((n_new, n_new, LANES), lambda i, c: (0, 0, c)),
                  pl.BlockSpec((1, LANES), lambda i, c: (0, c))],
        out_specs=pl.BlockSpec((n_new, nb, LANES), lambda i, c: (0, i, c)),
        out_shape=jax.ShapeDtypeStruct((n_new, b, ch), F32),
        compiler_params=_params(("arbitrary", "arbitrary"), 32),
        name="conv_sample",
    )(state, glu3, wh, wn, conv_b.reshape(1, ch)).transpose(1, 0, 2)


ATTN_UNROLL = 16
MERGE_ROWS = 256
MERGE_UNROLL = 4


def _head_norm(o, g):
    return o * lax.rsqrt(jnp.mean(o * o, axis=-1, keepdims=True) + EPS) * g


def _attn_prompt_kernel(q_ref, k_ref, v_ref, g_ref, o_ref, acc_scr, m_scr, l_scr):
    seq = q_ref.shape[1]
    qi = lax.broadcasted_iota(jnp.int32, (Q_BLOCK, 2 * Q_BLOCK), 0)
    kj2 = lax.broadcasted_iota(jnp.int32, (Q_BLOCK, 2 * Q_BLOCK), 1)

    for pat, (win, dil) in enumerate(zip(WINDOWS, DILATIONS)):
        assert win // dil == Q_BLOCK
        n_blk = seq // dil // Q_BLOCK

        def rows(ref, start, dil=dil):
            if dil == 1:
                return ref[0, pl.ds(start, Q_BLOCK), :]
            return ref[0, pl.ds(start, Q_BLOCK, stride=dil), :]

        def put(ref, start, val, dil=dil, pat=pat):
            if dil == 1:
                ref[pat, pl.ds(start, Q_BLOCK), :] = val
            else:
                ref[pat, pl.ds(start, Q_BLOCK, stride=dil), :] = val

        def body(i, carry, dil=dil, n_blk=n_blk, rows=rows, put=put):
            r = i // n_blk
            n = i % n_blk
            cur = r + dil * Q_BLOCK * n
            prev = r + dil * Q_BLOCK * jnp.maximum(n - 1, 0)
            if dil == 1:
                cur = pl.multiple_of(cur, Q_BLOCK)
                prev = pl.multiple_of(prev, Q_BLOCK)
            qb = rows(q_ref, cur).astype(BF16)
            k2 = jnp.concatenate([rows(k_ref, prev), rows(k_ref, cur)], axis=0).astype(BF16)
            v2 = jnp.concatenate([rows(v_ref, prev), rows(v_ref, cur)], axis=0).astype(BF16)
            has_prev = jnp.minimum(n, 1)
            lo = qi * has_prev + Q_BLOCK * (1 - has_prev)
            s = _nt_dot(qb, k2) * ATT_SCALE
            s = jnp.where(kj2 >= lo, jnp.where(kj2 <= qi + Q_BLOCK, s, NEG_INF), NEG_INF)
            m = jnp.max(s, axis=-1, keepdims=True)
            p = jnp.exp(s - m).astype(BF16)
            v_ext = jnp.concatenate([v2, jnp.ones_like(v2)], axis=1)
            acc = jnp.dot(p, v_ext, preferred_element_type=F32)
            put(acc_scr, cur, acc[:, :HEAD_DIM])
            put(m_scr, cur, jnp.broadcast_to(m, (Q_BLOCK, HEAD_DIM)))
            put(l_scr, cur, acc[:, HEAD_DIM:])
            return carry

        lax.fori_loop(0, dil * n_blk, body, 0, unroll=ATTN_UNROLL)

    n_pat = len(WINDOWS)

    def merge(c, carry):
        sl = pl.ds(pl.multiple_of(c * MERGE_ROWS, MERGE_ROWS), MERGE_ROWS)
        ms = [m_scr[p, sl, :] for p in range(n_pat)]
        top = functools.reduce(jnp.maximum, ms)
        ws = [jnp.exp(m - top) for m in ms]
        num = functools.reduce(jnp.add, [w * acc_scr[p, sl, :] for p, w in enumerate(ws)])
        den = functools.reduce(jnp.add, [w * l_scr[p, sl, :] for p, w in enumerate(ws)])
        o_ref[0, sl, :] = _head_norm(num / den, g_ref[0])
        return carry

    lax.fori_loop(0, seq // MERGE_ROWS, merge, 0, unroll=MERGE_UNROLL)


def _attn_prompt(q3, k3, v3, head_g):
    b, seq, _ = q3.shape
    spec = pl.BlockSpec((1, seq, HEAD_DIM), lambda i, h: (i, 0, h))
    return pl.pallas_call(
        _attn_prompt_kernel,
        grid=(b, N_HEADS),
        in_specs=[spec, spec, spec, pl.BlockSpec((1, 1, HEAD_DIM), lambda i, h: (h, 0, 0))],
        out_specs=spec,
        out_shape=jax.ShapeDtypeStruct((b, seq, ATT_W), F32),
        scratch_shapes=[pltpu.VMEM((len(WINDOWS), seq, HEAD_DIM), F32)] * 3,
        compiler_params=_params(("arbitrary", "arbitrary"), 32),
        name="attn_prompt",
    )(q3, k3, v3, head_g.reshape(N_HEADS, 1, HEAD_DIM))


S_PAD = 8
TAIL = 512
N_COLS = LANES


def _attn_sample_kernel(q_ref, kn_ref, vn_ref, kt_ref, vt_ref, kd_ref, vd_ref, g_ref, o_ref,
                        k_scr, v_scr, *, n_new, past):
    n_dil = past // DILATIONS[2]
    kd = jnp.swapaxes(kd_ref[0], 0, 1)
    vd = jnp.swapaxes(vd_ref[0], 0, 1)
    kt = jnp.swapaxes(kt_ref[0].reshape(TAIL, N_HEADS, HEAD_DIM), 0, 1)
    vt = jnp.swapaxes(vt_ref[0].reshape(TAIL, N_HEADS, HEAD_DIM), 0, 1)
    for h in range(N_HEADS):
        lanes = slice(h * HEAD_DIM, (h + 1) * HEAD_DIM)
        k_scr[0:TAIL, lanes] = kt[h].astype(BF16)
        v_scr[0:TAIL, lanes] = vt[h].astype(BF16)
        for res in range(n_new):
            dst = slice(TAIL + res * n_dil, TAIL + (res + 1) * n_dil)
            k_scr[dst, lanes] = kd[res * N_HEADS + h].astype(BF16)
            v_scr[dst, lanes] = vd[res * N_HEADS + h].astype(BF16)
    dil0 = TAIL

    q8 = q_ref[0]
    qt = jnp.concatenate([q8] * N_HEADS + [jnp.zeros_like(q8)] * (N_COLS // S_PAD - N_HEADS), axis=0)
    row = lax.broadcasted_iota(jnp.int32, (N_COLS, ATT_W), 0)
    lane = lax.broadcasted_iota(jnp.int32, (N_COLS, ATT_W), 1)
    qbd = jnp.where(row // S_PAD == lane // HEAD_DIM, qt, 0.0).astype(BF16)

    def col_query(shape):
        return lax.broadcasted_iota(jnp.int32, shape, 1) % S_PAD % n_new

    def key_row(shape):
        return lax.broadcasted_iota(jnp.int32, shape, 0)

    groups = []
    s = _nt_dot(k_scr[0:TAIL, :], qbd) * ATT_SCALE
    r, c = key_row(s.shape), col_query(s.shape)
    base = past - TAIL
    mult = ((((base + r) % DILATIONS[1] == (past + c) % DILATIONS[1])
             & (base + r >= past + c - WINDOWS[1])).astype(F32)
            + (base + r >= past + c - WINDOWS[0]).astype(F32))
    groups.append((s, mult, v_scr[0:TAIL, :]))
    for res in range(n_new):
        rows = slice(dil0 + res * n_dil, dil0 + (res + 1) * n_dil)
        s = _nt_dot(k_scr[rows, :], qbd) * ATT_SCALE
        mult = (col_query(s.shape) == res).astype(F32)
        groups.append((s, mult, v_scr[rows, :]))
    s = _nt_dot(kn_ref[0].astype(BF16), qbd) * ATT_SCALE
    r, c = key_row(s.shape), col_query(s.shape)
    mult = (r <= c).astype(F32) + 2.0 * (r == c).astype(F32)
    groups.append((s, mult, vn_ref[0].astype(BF16)))

    m = None
    for s, mult, _ in groups:
        gm = jnp.max(jnp.where(mult > 0, s, NEG_INF), axis=0, keepdims=True)
        m = gm if m is None else jnp.maximum(m, gm)
    num = jnp.zeros((N_COLS, ATT_W), F32)
    den = jnp.zeros((N_COLS, LANES), F32)
    for s, mult, v in groups:
        p = mult * jnp.exp(jnp.where(mult > 0, s - m, NEG_INF))
        pt = p.T.astype(BF16)
        num = num + jnp.dot(pt, v, preferred_element_type=F32)
        den = den + jnp.dot(pt, jnp.ones((v.shape[0], LANES), BF16), preferred_element_type=F32)
    outs = []
    for h in range(N_HEADS):
        o = (num[h * S_PAD:(h + 1) * S_PAD, h * HEAD_DIM:(h + 1) * HEAD_DIM]
             / den[h * S_PAD:(h + 1) * S_PAD, :])
        outs.append(_head_norm(o, g_ref[h]))
    o_ref[0] = jnp.concatenate(outs, axis=1)


def _attn_sample(q8, k8, v8, cache_k, cache_v, head_g, n_new, past):
    b = q8.shape[0]
    lw = cache_k.shape[1]
    assert lw == past and lw % 16 == 0 and lw >= WINDOWS[2] and n_new <= 4
    dil = DILATIONS[2]
    ck_rows = cache_k.reshape(b, lw * N_HEADS, HEAD_DIM)
    cv_rows = cache_v.reshape(b, lw * N_HEADS, HEAD_DIM)
    ck_grp = cache_k.reshape(b, lw // dil, dil * N_HEADS, HEAD_DIM)
    cv_grp = cache_v.reshape(b, lw // dil, dil * N_HEADS, HEAD_DIM)
    new_spec = pl.BlockSpec((1, S_PAD, ATT_W), lambda i: (i, 0, 0))
    tail_spec = pl.BlockSpec((1, TAIL * N_HEADS, HEAD_DIM), lambda i: (i, lw // TAIL - 1, 0))
    dil_spec = pl.BlockSpec((1, lw // dil, n_new * N_HEADS, HEAD_DIM), lambda i: (i, 0, 0, 0))
    n_keys = TAIL + n_new * (lw // dil)
    return pl.pallas_call(
        functools.partial(_attn_sample_kernel, n_new=n_new, past=past),
        grid=(b,),
        in_specs=[new_spec, new_spec, new_spec, tail_spec, tail_spec, dil_spec, dil_spec,
                  pl.BlockSpec((N_HEADS, 1, HEAD_DIM), lambda i: (0, 0, 0))],
        out_specs=new_spec,
        out_shape=jax.ShapeDtypeStruct((b, S_PAD, ATT_W), F32),
        scratch_shapes=[pltpu.VMEM((n_keys, ATT_W), BF16), pltpu.VMEM((n_keys, ATT_W), BF16)],
        compiler_params=_params(("arbitrary",), 48),
        name="attn_sample",
    )(q8, k8, v8, ck_rows, cv_rows, ck_grp, cv_grp, head_g.reshape(N_HEADS, 1, HEAD_DIM))


def _finish_kernel(x_ref, conv_ref, attn_ref, g1_ref, sh2_ref, sc2_ref, lng_ref, lnb_ref, n2g_ref,
                   wout_ref, wq_ref, x1_ref, h2t_ref, qp_ref, cat_scr):
    y = conv_ref[...]
    mu = jnp.mean(y, axis=-1, keepdims=True)
    yc = y - mu
    var = jnp.mean(yc * yc, axis=-1, keepdims=True)
    z = yc * lax.rsqrt(var + EPS) * lng_ref[...] + lnb_ref[...]
    cat_scr[:, :CONV_CH] = (z * jax.nn.sigmoid(z)).astype(BF16)
    cat_scr[:, CONV_CH:] = attn_ref[...].astype(BF16)
    mix = jnp.dot(cat_scr[...], wout_ref[...], preferred_element_type=F32)
    x1 = x_ref[...] + (1.0 + _rows(g1_ref)) * mix
    x1_ref[...] = x1
    n = x1 * lax.rsqrt(jnp.mean(x1 * x1, axis=-1, keepdims=True) + EPS)
    h2 = (n * n2g_ref[...]) * (1.0 + _rows(sc2_ref)) + _rows(sh2_ref)
    h2t_ref[...] = h2.T.astype(BF16)
    qp_ref[...] = jnp.dot(h2.astype(BF16), wq_ref[...], preferred_element_type=F32).astype(BF16)


def _finish(x, conv_raw, attn, mods, group_rows, ln_g, ln_b, n2_g, wout_bf, wq_bf, tm):
    t = x.shape[0]
    row = lambda n: pl.BlockSpec((1, n), lambda i: (0, 0))
    const = lambda shape: pl.BlockSpec(shape, lambda i: (0, 0), pipeline_mode=pl.Buffered(1))
    tile = lambda n: pl.BlockSpec((tm, n), lambda i: (i, 0))
    return pl.pallas_call(
        _finish_kernel,
        grid=(t // tm,),
        in_specs=[tile(D_MODEL), tile(CONV_CH), tile(ATT_W),
                  _mod_spec(mods, 2, tm, group_rows, 1),
                  _mod_spec(mods, 3, tm, group_rows, 1),
                  _mod_spec(mods, 4, tm, group_rows, 1),
                  row(CONV_CH), row(CONV_CH), row(D_MODEL),
                  const((D_MODEL, D_MODEL)), const((D_MODEL, D_MODEL))],
        out_specs=[tile(D_MODEL), pl.BlockSpec((D_MODEL, tm), lambda i: (0, i)), tile(D_MODEL)],
        out_shape=[jax.ShapeDtypeStruct((t, D_MODEL), F32),
                   jax.ShapeDtypeStruct((D_MODEL, t), BF16),
                   jax.ShapeDtypeStruct((t, D_MODEL), BF16)],
        scratch_shapes=[pltpu.VMEM((tm, D_MODEL), BF16)],
        compiler_params=_params(("arbitrary",), 56),
        name="finish",
    )(x, conv_raw, attn, mods, mods, mods, ln_g.reshape(1, CONV_CH), ln_b.reshape(1, CONV_CH),
      n2_g.reshape(1, D_MODEL), wout_bf, wq_bf)


TOPK_RANK = PEER_TOPK + 1
TOPK_ROWS = -(-TOPK_RANK // SUBLANES) * SUBLANES


def _all_sublanes(x, combine):
    shift = SUBLANES // 2
    while shift:
        x = combine(x, pltpu.roll(x, shift, 0))
        shift //= 2
    return x


def _sorting_network(n):
    pairs = []

    def merge(lo, length, r):
        step = 2 * r
        if step < length:
            merge(lo, length, step)
            merge(lo + r, length, step)
            pairs.extend((i, i + r) for i in range(lo + r, lo + length - r, step))
        else:
            pairs.append((lo, lo + r))

    def sort(lo, length):
        if length > 1:
            sort(lo, length // 2)
            sort(lo + length // 2, length // 2)
            merge(lo, length, 1)

    sort(0, n)
    return pairs


def _top_values(s, k, out_scr):
    n_tiles = s.shape[0] // SUBLANES
    v = [s[r * SUBLANES:(r + 1) * SUBLANES, :] for r in range(n_tiles)]
    for i, j in _sorting_network(n_tiles):
        v[i], v[j] = jnp.maximum(v[i], v[j]), jnp.minimum(v[i], v[j])
    out_scr[...] = jnp.full(out_scr.shape, NEG_INF, F32)
    depth = min(n_tiles, k)
    v = v[:depth]
    for i in range(k):
        m = _all_sublanes(v[0], jnp.maximum)
        out_scr[i:i + 1, :] = m[0:1, :]
        pop = v[0] == m
        live = min(depth, k - i)
        for r in range(live):
            below = v[r + 1] if r + 1 < depth else NEG_INF
            v[r] = jnp.where(pop, below, v[r])


def _pair_candidates(a_scr, b_scr):
    tb = a_scr.shape[1]
    rows_all = lax.broadcasted_iota(jnp.int32, (TOPK_ROWS, tb), 0)
    rows_one = lax.broadcasted_iota(jnp.int32, (SUBLANES, tb), 0)
    cands = [a_scr[0:1, :] + b_scr[...],
             jnp.where(rows_all >= 1, b_scr[0:1, :] + a_scr[...], NEG_INF)]
    single = []
    for i in range(1, TOPK_RANK):
        j_max = TOPK_RANK // (i + 1) - 1
        if j_max >= 2:
            assert j_max < SUBLANES
            ok = (rows_one >= 1) & (rows_one <= j_max)
            cands.append(jnp.where(ok, a_scr[i:i + 1, :] + b_scr[0:SUBLANES, :], NEG_INF))
        elif j_max == 1:
            single.append(i)
    if single:
        assert single == list(range(single[0], single[-1] + 1)) and single[-1] < SUBLANES
        ok = (rows_one >= single[0]) & (rows_one <= single[-1])
        cands.append(jnp.where(ok, b_scr[1:2, :] + a_scr[0:SUBLANES, :], NEG_INF))
    return [c[r:r + SUBLANES, :] for c in cands for r in range(0, c.shape[0], SUBLANES)]


def _column_reduce(tiles, combine):
    return _all_sublanes(functools.reduce(combine, tiles), combine)


def _topk_kernel(qp_ref, keys_ref, s1_ref, s2_ref, e1_ref, e2_ref, thr_ref, a_scr, b_scr):
    tb = qp_ref.shape[0]
    for h in range(PEER_HEADS):
        sc = []
        for p in range(2):
            col = (h * 2 + p) * PEER_DK
            sc.append(_nt_dot(keys_ref[h, p], qp_ref[:, col:col + PEER_DK]))
        s1, s2 = sc
        _top_values(s1, TOPK_RANK, a_scr)
        _top_values(s2, TOPK_RANK, b_scr)
        cands = _pair_candidates(a_scr, b_scr)
        top = None
        cum = jnp.zeros((SUBLANES, tb), F32)
        z = jnp.zeros((SUBLANES, tb), F32)
        v_in = jnp.zeros((SUBLANES, tb), F32)
        v_out = jnp.zeros((SUBLANES, tb), F32)
        for _ in range(TOPK_RANK):
            m = _column_reduce(cands, jnp.maximum)
            hits = [c == m for c in cands]
            cnt = _column_reduce([jnp.where(e, 1.0, 0.0) for e in hits], jnp.add)
            cands = [jnp.where(e, NEG_INF, c) for e, c in zip(hits, cands)]
            top = m if top is None else top
            take = jnp.minimum(cnt, jnp.maximum(PEER_TOPK - cum, 0.0))
            z = z + take * jnp.exp(m - top)
            reached = cum + cnt
            v_in = jnp.where((cum < PEER_TOPK) & (reached >= PEER_TOPK), m, v_in)
            v_out = jnp.where((cum < TOPK_RANK) & (reached >= TOPK_RANK), m, v_out)
            cum = reached
        s1_ref[h] = s1
        s2_ref[h] = s2
        e1_ref[h] = jnp.exp(s1 - a_scr[0:1, :])
        e2_ref[h] = jnp.exp(s2 - b_scr[0:1, :]) / z[0:1, :]
        thr_ref[h] = 0.5 * (v_in[0:1, :] + v_out[0:1, :])


def _topk(qp, keys_bf, tb):
    t = qp.shape[0]
    big = pl.BlockSpec((PEER_HEADS, PEER_NKEYS, tb), lambda i: (0, 0, i))
    big_sds = jax.ShapeDtypeStruct((PEER_HEADS, PEER_NKEYS, t), F32)
    return pl.pallas_call(
        _topk_kernel,
        grid=(t // tb,),
        in_specs=[pl.BlockSpec((tb, D_MODEL), lambda i: (i, 0)),
                  pl.BlockSpec((PEER_HEADS, 2, PEER_NKEYS, PEER_DK), lambda i: (0, 0, 0, 0))],
        out_specs=[big, big, big, big, pl.BlockSpec((PEER_HEADS, 1, tb), lambda i: (0, 0, i))],
        out_shape=[big_sds, big_sds, big_sds, big_sds, jax.ShapeDtypeStruct((PEER_HEADS, 1, t), F32)],
        scratch_shapes=[pltpu.VMEM((TOPK_ROWS, tb), F32), pltpu.VMEM((TOPK_ROWS, tb), F32)],
        compiler_params=_params(("arbitrary",), 32),
        name="peer_topk",
    )(qp, keys_bf)


_SQRT_HALF = float(np.sqrt(0.5))


def _gelu(x):
    return 0.5 * x * (1.0 + lax.erf(x * _SQRT_HALF))


MXU_TILE = 256
HID_GROUPS = (512, 512)


def _peer_kernel(ht_ref, u_ref, vt_ref, s1_ref, s2_ref, e1_ref, e2_ref, thr_ref, o_ref, acc_scr, *w_refs):
    j = pl.program_id(1)
    tb = ht_ref.shape[1]
    ec = u_ref.shape[0]
    assert ec == SUBLANES * PEER_NKEYS
    n_k, n_n = ec // MXU_TILE, tb // MXU_TILE
    assert len(w_refs) == n_k * n_n

    @pl.when(j == 0)
    def _():
        acc_scr[...] = jnp.zeros_like(acc_scr)

    row0 = pl.multiple_of(j * SUBLANES, SUBLANES)
    assert sum(HID_GROUPS) == ec
    starts = [sum(HID_GROUPS[:g]) for g in range(len(HID_GROUPS))]
    hids = [jnp.dot(u_ref[s:s + n, :], ht_ref[...], preferred_element_type=F32)
            for s, n in zip(starts, HID_GROUPS)]
    per_tile = MXU_TILE // PEER_NKEYS
    for k in range(n_k):
        for n in range(n_n):
            w_ref = w_refs[k * n_n + n]
            for a in range(per_tile):
                ii = k * per_tile + a
                q = max(g for g, s in enumerate(starts) if s <= ii * PEER_NKEYS)
                r = ii * PEER_NKEYS - starts[q]
                for b in range(MXU_TILE // LANES):
                    lc = n * (MXU_TILE // LANES) + b
                    sl = slice(lc * LANES, (lc + 1) * LANES)
                    gate = jnp.zeros((PEER_NKEYS, LANES), F32)
                    for h in range(PEER_HEADS):
                        s1_rows = s1_ref[h, pl.ds(row0, SUBLANES), sl]
                        e1_rows = e1_ref[h, pl.ds(row0, SUBLANES), sl]
                        bound = thr_ref[h, :, sl] - s1_rows[ii:ii + 1, :]
                        val = e2_ref[h, :, sl] * e1_rows[ii:ii + 1, :]
                        gate = gate + jnp.where(s2_ref[h, :, sl] >= bound, val, 0.0)
                    hid = hids[q][r:r + PEER_NKEYS, sl]
                    w_ref[a * PEER_NKEYS:(a + 1) * PEER_NKEYS, b * LANES:(b + 1) * LANES] = (
                        gate * _gelu(hid)).astype(BF16)
    w_all = jnp.concatenate([jnp.concatenate([w_refs[k * n_n + n][...] for n in range(n_n)], axis=1)
                             for k in range(n_k)], axis=0)
    acc_scr[...] += jnp.dot(vt_ref[...], w_all, preferred_element_type=F32)

    @pl.when(j == pl.num_programs(1) - 1)
    def _():
        o_ref[...] = acc_scr[...].T


def _peer(h2t, u_bf, vt_bf, s1, s2, e1, e2, thr, tb, ec):
    t = h2t.shape[1]
    n_chunks = u_bf.shape[0] // ec
    big = pl.BlockSpec((PEER_HEADS, PEER_NKEYS, tb), lambda i, j: (0, 0, i))
    n_tiles = (ec // MXU_TILE) * (tb // MXU_TILE)
    return pl.pallas_call(
        _peer_kernel,
        grid=(t // tb, n_chunks),
        in_specs=[pl.BlockSpec((D_MODEL, tb), lambda i, j: (0, i)),
                  pl.BlockSpec((ec, D_MODEL), lambda i, j: (j, 0)),
                  pl.BlockSpec((D_MODEL, ec), lambda i, j: (0, j)),
                  big, big, big, big,
                  pl.BlockSpec((PEER_HEADS, 1, tb), lambda i, j: (0, 0, i))],
        out_specs=pl.BlockSpec((tb, D_MODEL), lambda i, j: (i, 0)),
        out_shape=jax.ShapeDtypeStruct((t, D_MODEL), F32),
        scratch_shapes=[pltpu.VMEM((D_MODEL, tb), F32)] + [pltpu.VMEM((MXU_TILE, MXU_TILE), BF16)] * n_tiles,
        compiler_params=_params(("arbitrary", "arbitrary"), 56),
        name="peer",
    )(h2t, u_bf, vt_bf, s1, s2, e1, e2, thr)


def _final_kernel(x1_ref, p_ref, g2_ref, fg_ref, y_ref):
    x2 = x1_ref[...] + (1.0 + _rows(g2_ref)) * p_ref[...]
    y_ref[...] = x2 * lax.rsqrt(jnp.mean(x2 * x2, axis=-1, keepdims=True) + EPS) * fg_ref[...]


def _final(x1, peer_out, mods, group_rows, final_g, tm):
    t = x1.shape[0]
    tile = pl.BlockSpec((tm, D_MODEL), lambda i: (i, 0))
    return pl.pallas_call(
        _final_kernel,
        grid=(t // tm,),
        in_specs=[tile, tile, _mod_spec(mods, 5, tm, group_rows, 1),
                  pl.BlockSpec((1, D_MODEL), lambda i: (0, 0))],
        out_specs=tile,
        out_shape=jax.ShapeDtypeStruct((t, D_MODEL), F32),
        compiler_params=_params(("arbitrary",), 48),
        name="final",
    )(x1, peer_out, mods, final_g.reshape(1, D_MODEL))


PAST_LEN = 2048
MIXIN_TOKENS = 256
TOKEN_TILE = 256
PEER_TOKENS = 512
PEER_CHUNK = 1024
TOPK_TOKENS = 256


def _channel_tail(x, conv_raw, attn, mods, group_rows, weights):
    x1, h2t, qp = _finish(x, conv_raw, attn, mods, group_rows, weights["ln_g"], weights["ln_b"],
                          weights["n2_g"], weights["wout"], weights["wq"], TOKEN_TILE)
    s1, s2, e1, e2, thr = _topk(qp, weights["keys"], TOPK_TOKENS)
    peer_out = _peer(h2t, weights["u"], weights["vt"], s1, s2, e1, e2, thr, PEER_TOKENS, PEER_CHUNK)
    return _final(x1, peer_out, mods, group_rows, weights["final_g"], TOKEN_TILE)


def kernel(x_prompt, x_sample, cache_k, cache_v, state_conv, c_prompt, c_sample, w_ada, b_ada, norm1_g, w_in,
           conv_w, conv_b, conv_ln_g, conv_ln_b, attn_out_g, w_out, norm2_g, peer_wq, peer_keys, peer_u, peer_v,
           final_g):
    b, seq, d = x_prompt.shape
    db, n_new, _ = x_sample.shape
    depth = w_ada.shape[0]
    assert depth == 1 and d == D_MODEL
    cache_len = cache_k.shape[2]
    n_sample = db * n_new
    assert seq % MIXIN_TOKENS == 0 and n_sample % PEER_TOKENS == 0

    c_rows = jnp.concatenate([c_prompt, jnp.repeat(c_sample, n_new, axis=0)], axis=0)
    mods = _ada(c_rows, w_ada[0], b_ada[0])
    mods_p = mods[:b].reshape(b, 1, N_ADA * D_MODEL)
    mods_s = mods[b:]

    weights = {
        "ln_g": conv_ln_g[0], "ln_b": conv_ln_b[0], "n2_g": norm2_g[0],
        "wout": w_out[0].astype(BF16), "wq": peer_wq[0].astype(BF16),
        "keys": peer_keys[0].astype(BF16), "u": peer_u[0].astype(BF16),
        "vt": peer_v[0].T.astype(BF16), "final_g": final_g,
    }
    w_in_bf = w_in[0].astype(BF16)

    xp = x_prompt.reshape(b * seq, d)
    tables_p = _rope_tables(np.arange(seq))
    glu, q, k, v = _mixin(xp, mods_p, seq, norm1_g[0], w_in_bf, tables_p, MIXIN_TOKENS)
    glu3 = glu.reshape(b, seq, CONV_CH)
    conv_raw = _conv_prompt(glu3, jnp.zeros((b, CONV_PAD, CONV_CH), F32), conv_w[0], conv_b[0])
    attn = _attn_prompt(q.reshape(b, seq, ATT_W), k.reshape(b, seq, ATT_W), v.reshape(b, seq, ATT_W),
                        attn_out_g[0])
    y_prompt = _channel_tail(xp, conv_raw.reshape(b * seq, CONV_CH), attn.reshape(b * seq, ATT_W),
                             mods_p, seq, weights).reshape(b, seq, d)
    keep = min(WINDOWS[2], seq)
    new_k_prompt = k.reshape(b, seq, N_HEADS, HEAD_DIM)[:, seq - keep:][None]
    new_v_prompt = v.reshape(b, seq, N_HEADS, HEAD_DIM)[:, seq - keep:][None]
    new_conv_prompt = glu3[:, seq - CONV_HIST:][None]

    xs = x_sample.reshape(n_sample, d)
    tables_s = tuple(jnp.tile(t, (TOKEN_TILE // n_new, 1)) for t in _rope_tables(PAST_LEN + np.arange(n_new)))
    glu_s, q_s, k_s, v_s = _mixin(xs, mods_s, None, norm1_g[0], w_in_bf, tables_s, TOKEN_TILE)
    glu_s3 = glu_s.reshape(db, n_new, CONV_CH)
    conv_raw_s = _conv_sample(state_conv[0], glu_s3, conv_w[0], conv_b[0])
    pad8 = lambda a: jnp.pad(a.reshape(db, n_new, ATT_W), ((0, 0), (0, S_PAD - n_new), (0, 0)))
    attn_s = _attn_sample(pad8(q_s), pad8(k_s), pad8(v_s), cache_k[0], cache_v[0], attn_out_g[0], n_new,
                          cache_len)
    attn_s = attn_s[:, :n_new].reshape(n_sample, ATT_W)
    y_sample = _channel_tail(xs, conv_raw_s.reshape(n_sample, CONV_CH), attn_s,
                             mods_s, None, weights).reshape(db, n_new, d)
    new_k_sample = k_s.reshape(db, n_new, N_HEADS, HEAD_DIM)[None]
    new_v_sample = v_s.reshape(db, n_new, N_HEADS, HEAD_DIM)[None]
    new_conv_sample = jnp.concatenate([state_conv[0], glu_s3], axis=1)[:, n_new:][None]

    return (y_prompt, y_sample, new_k_prompt, new_v_prompt, new_conv_prompt,
            new_k_sample, new_v_sample, new_conv_sample)
```

```python
import functools

import numpy as np
import jax
import jax.numpy as jnp
from jax import lax
from jax.experimental import pallas as pl
from jax.experimental.pallas import tpu as pltpu

F32 = jnp.float32
BF16 = jnp.bfloat16

D_MODEL = 2048
CONV_CH = 1024
CONV_W = 31
N_HEADS = 8
HEAD_DIM = 128
ATT_W = N_HEADS * HEAD_DIM
ROT_DIM = HEAD_DIM // 4
ROPE_THETA = 500000.0
WINDOWS = (128, 512, 2048)
DILATIONS = (1, 4, 16)
Q_BLOCK = 128
ATT_SCALE = HEAD_DIM ** -0.5
N_ADA = 6
PEER_HEADS = 8
PEER_NKEYS = 128
PEER_DK = 128
PEER_TOPK = 16
EPS = 1e-6

LANES = 128
SUBLANES = 8
MIB = 1024 * 1024

NEG_INF = float("-inf")


def _nt_dot(a, b):
    return lax.dot_general(a, b, (((1,), (1,)), ((), ())), preferred_element_type=F32)


def _rows(ref):
    return ref[0] if len(ref.shape) == 3 else ref[...]


def _params(sem, vmem_mib):
    return pltpu.CompilerParams(dimension_semantics=sem, vmem_limit_bytes=vmem_mib * MIB)


def _ada_kernel(c_ref, w_ref, b_ref, o_ref):
    c = c_ref[...]
    a = (c * jax.nn.sigmoid(c)).astype(BF16)
    o_ref[...] = jnp.dot(a, w_ref[...].astype(BF16), preferred_element_type=F32) + b_ref[...]


def _ada(c_rows, w_ada, b_ada):
    rows = c_rows.shape[0]
    n = w_ada.shape[1]
    tn = 1024
    return pl.pallas_call(
        _ada_kernel,
        grid=(n // tn,),
        in_specs=[pl.BlockSpec((rows, D_MODEL), lambda j: (0, 0)),
                  pl.BlockSpec((D_MODEL, tn), lambda j: (0, j)),
                  pl.BlockSpec((1, tn), lambda j: (0, j))],
        out_specs=pl.BlockSpec((rows, tn), lambda j: (0, j)),
        out_shape=jax.ShapeDtypeStruct((rows, n), F32),
        compiler_params=_params(("arbitrary",), 48),
        name="ada",
    )(c_rows, w_ada, b_ada.reshape(1, n))


def _rope_heads(z, cos, sin_lo, sin_hi):
    outs = []
    for h in range(N_HEADS):
        zh = z[:, h * HEAD_DIM:(h + 1) * HEAD_DIM]
        up = pltpu.roll(zh, HEAD_DIM - ROT_DIM // 2, 1)
        dn = pltpu.roll(zh, ROT_DIM // 2, 1)
        outs.append(zh * cos + up * sin_lo + dn * sin_hi)
    return jnp.concatenate(outs, axis=1)


def _mixin_kernel(x_ref, sh_ref, sc_ref, g_ref, w_ref, cos_ref, slo_ref, shi_ref,
                  glu_ref, q_ref, k_ref, v_ref):
    x = x_ref[...]
    y = x * lax.rsqrt(jnp.mean(x * x, axis=-1, keepdims=True) + EPS)
    h = ((y * g_ref[...]) * (1.0 + _rows(sc_ref)) + _rows(sh_ref)).astype(BF16)

    def group(c):
        return jnp.dot(h, w_ref[:, c * ATT_W:(c + 1) * ATT_W], preferred_element_type=F32)

    glu_ref[...] = group(0) * jax.nn.sigmoid(group(1))
    q_ref[...] = _rope_heads(group(2), cos_ref[...], slo_ref[...], shi_ref[...])
    k_ref[...] = _rope_heads(group(3), cos_ref[...], slo_ref[...], shi_ref[...])
    v_ref[...] = group(4)


def _rope_tables(pos):
    inv = ROPE_THETA ** (-np.arange(0, ROT_DIM, 2, dtype=np.float32) / ROT_DIM)
    ang = (pos.astype(np.float32)[:, None] * inv[None, :]).astype(np.float32)
    cos, sin = np.cos(ang), np.sin(ang)
    half = ROT_DIM // 2
    n = pos.shape[0]
    c = np.ones((n, HEAD_DIM), np.float32)
    c[:, :half] = cos
    c[:, half:ROT_DIM] = cos
    s_lo = np.zeros((n, HEAD_DIM), np.float32)
    s_lo[:, :half] = -sin
    s_hi = np.zeros((n, HEAD_DIM), np.float32)
    s_hi[:, half:ROT_DIM] = sin
    return jnp.asarray(c), jnp.asarray(s_lo), jnp.asarray(s_hi)


def _mod_spec(mods, col, tm, group_rows, ngrid):
    if mods.ndim == 3:
        assert group_rows % tm == 0
        per_group = group_rows // tm
        if ngrid == 1:
            return pl.BlockSpec((1, 1, D_MODEL), lambda i: (i // per_group, 0, col))
        return pl.BlockSpec((1, 1, D_MODEL), lambda i, j: (i // per_group, 0, col))
    if ngrid == 1:
        return pl.BlockSpec((tm, D_MODEL), lambda i: (i, col))
    return pl.BlockSpec((tm, D_MODEL), lambda i, j: (i, col))


def _mixin(x, mods, group_rows, norm_g, w_in_bf, tables, tm):
    t = x.shape[0]
    cos, s_lo, s_hi = tables
    pos_blocks = cos.shape[0] // tm
    assert w_in_bf.shape[1] == 5 * ATT_W and CONV_CH == ATT_W
    tab_spec = pl.BlockSpec((tm, HEAD_DIM), lambda i: (i % pos_blocks, 0))
    out_spec = pl.BlockSpec((tm, ATT_W), lambda i: (i, 0))
    out_sds = jax.ShapeDtypeStruct((t, ATT_W), F32)
    return pl.pallas_call(
        _mixin_kernel,
        grid=(t // tm,),
        in_specs=[pl.BlockSpec((tm, D_MODEL), lambda i: (i, 0)),
                  _mod_spec(mods, 0, tm, group_rows, 1),
                  _mod_spec(mods, 1, tm, group_rows, 1),
                  pl.BlockSpec((1, D_MODEL), lambda i: (0, 0)),
                  pl.BlockSpec(w_in_bf.shape, lambda i: (0, 0), pipeline_mode=pl.Buffered(1)),
                  tab_spec, tab_spec, tab_spec],
        out_specs=[out_spec, out_spec, out_spec, out_spec],
        out_shape=[out_sds, out_sds, out_sds, out_sds],
        compiler_params=_params(("arbitrary",), 56),
        name="mixin",
    )(x, mods, mods, norm_g.reshape(1, D_MODEL), w_in_bf, cos, s_lo, s_hi)


CONV_HIST = CONV_W - 1
CONV_PAD = 32
CONV_ROWS = 64
CONV_UNROLL = 2


def _conv_prompt_kernel(g_ref, hist_ref, w_ref, b_ref, o_ref, xp_scr):
    seq = g_ref.shape[1]
    xp_scr[0:CONV_PAD, :] = hist_ref[0]
    xp_scr[CONV_PAD:CONV_PAD + seq, :] = g_ref[0]
    lead = CONV_PAD - CONV_HIST

    def body(c, carry):
        t0 = pl.multiple_of(c * CONV_ROWS, CONV_ROWS)
        win = xp_scr[pl.ds(t0, CONV_ROWS + CONV_PAD), :]
        acc = jnp.zeros((CONV_ROWS, LANES), F32) + b_ref[...]
        for sub in range(SUBLANES):
            shifted = win if sub == 0 else pltpu.roll(win, CONV_ROWS + CONV_PAD - sub, 0)
            for a in range(CONV_PAD // SUBLANES + 1):
                off = a * SUBLANES + sub
                j = off - lead
                if 0 <= j < CONV_W:
                    acc = acc + shifted[a * SUBLANES:a * SUBLANES + CONV_ROWS, :] * w_ref[j:j + 1, :]
        o_ref[0, pl.ds(t0, CONV_ROWS), :] = acc
        return carry

    lax.fori_loop(0, seq // CONV_ROWS, body, 0, unroll=CONV_UNROLL)


def _conv_prompt(glu3, hist_pad, conv_w, conv_b):
    b, seq, ch = glu3.shape
    return pl.pallas_call(
        _conv_prompt_kernel,
        grid=(b, ch // LANES),
        in_specs=[pl.BlockSpec((1, seq, LANES), lambda i, c: (i, 0, c)),
                  pl.BlockSpec((1, CONV_PAD, LANES), lambda i, c: (i, 0, c)),
                  pl.BlockSpec((CONV_W, LANES), lambda i, c: (0, c)),
                  pl.BlockSpec((1, LANES), lambda i, c: (0, c))],
        out_specs=pl.BlockSpec((1, seq, LANES), lambda i, c: (i, 0, c)),
        out_shape=jax.ShapeDtypeStruct((b, seq, ch), F32),
        scratch_shapes=[pltpu.VMEM((seq + CONV_PAD, LANES), F32)],
        compiler_params=_params(("arbitrary", "arbitrary"), 32),
        name="conv_prompt",
    )(glu3, hist_pad, conv_w, conv_b.reshape(1, ch))


def _conv_sample_kernel(hist_ref, g_ref, wh_ref, wn_ref, b_ref, o_ref):
    hist = hist_ref[...]
    g = g_ref[...]
    n_new = g.shape[1]
    for t in range(n_new):
        y = jnp.sum(hist * wh_ref[t][None], axis=1) + jnp.sum(g * wn_ref[t][None], axis=1)
        o_ref[t] = y + b_ref[...]


def _conv_sample(state, glu3, conv_w, conv_b):
    b, n_new, ch = glu3.shape
    wh = jnp.stack([jnp.pad(conv_w[:CONV_HIST - t], ((t, 0), (0, 0))) for t in range(n_new)])
    wn = jnp.stack([jnp.pad(conv_w[CONV_HIST - t:], ((0, n_new - 1 - t), (0, 0))) for t in range(n_new)])
    nb = 32
    return pl.pallas_call(
        _conv_sample_kernel,
        grid=(b // nb, ch // LANES),
        in_specs=[pl.BlockSpec((nb, CONV_HIST, LANES), lambda i, c: (i, 0, c)),
                  pl.BlockSpec((nb, n_new, LANES), lambda i, c: (i, 0, c)),
                  pl.BlockSpec((n_new, CONV_HIST, LANES), lambda i, c: (0, 0, c)),
                  pl.BlockSpec((n_new, n_new, LANES), lambda i, c: (0, 0, c)),
                  pl.BlockSpec((1, LANES), lambda i, c: (0, c))],
        out_specs=pl.BlockSpec((n_new, nb, LANES), lambda i, c: (0, i, c)),
        out_shape=jax.ShapeDtypeStruct((n_new, b, ch), F32),
        compiler_params=_params(("arbitrary", "arbitrary"), 32),
        name="conv_sample",
    )(state, glu3, wh, wn, conv_b.reshape(1, ch)).transpose(1, 0, 2)


ATTN_UNROLL = 16
MERGE_ROWS = 256
MERGE_UNROLL = 4


def _head_norm(o, g):
    return o * lax.rsqrt(jnp.mean(o * o, axis=-1, keepdims=True) + EPS) * g


def _attn_prompt_kernel(q_ref, k_ref, v_ref, g_ref, o_ref, acc_scr, m_scr, l_scr):
    seq = q_ref.shape[1]
    qi = lax.broadcasted_iota(jnp.int32, (Q_BLOCK, 2 * Q_BLOCK), 0)
    kj2 = lax.broadcasted_iota(jnp.int32, (Q_BLOCK, 2 * Q_BLOCK), 1)

    for pat, (win, dil) in enumerate(zip(WINDOWS, DILATIONS)):
        assert win // dil == Q_BLOCK
        n_blk = seq // dil // Q_BLOCK

        def rows(ref, start, dil=dil):
            if dil == 1:
                return ref[0, pl.ds(start, Q_BLOCK), :]
            return ref[0, pl.ds(start, Q_BLOCK, stride=dil), :]

        def put(ref, start, val, dil=dil, pat=pat):
            if dil == 1:
                ref[pat, pl.ds(start, Q_BLOCK), :] = val
            else:
                ref[pat, pl.ds(start, Q_BLOCK, stride=dil), :] = val

        def body(i, carry, dil=dil, n_blk=n_blk, rows=rows, put=put):
            r = i // n_blk
            n = i % n_blk
            cur = r + dil * Q_BLOCK * n
            prev = r + dil * Q_BLOCK * jnp.maximum(n - 1, 0)
            if dil == 1:
                cur = pl.multiple_of(cur, Q_BLOCK)
                prev = pl.multiple_of(prev, Q_BLOCK)
            qb = rows(q_ref, cur).astype(BF16)
            k2 = jnp.concatenate([rows(k_ref, prev), rows(k_ref, cur)], axis=0).astype(BF16)
            v2 = jnp.concatenate([rows(v_ref, prev), rows(v_ref, cur)], axis=0).astype(BF16)
            has_prev = jnp.minimum(n, 1)
            lo = qi * has_prev + Q_BLOCK * (1 - has_prev)
            s = _nt_dot(qb, k2) * ATT_SCALE
            s = jnp.where(kj2 >= lo, jnp.where(kj2 <= qi + Q_BLOCK, s, NEG_INF), NEG_INF)
            m = jnp.max(s, axis=-1, keepdims=True)
            p = jnp.exp(s - m).astype(BF16)
            v_ext = jnp.concatenate([v2, jnp.ones_like(v2)], axis=1)
            acc = jnp.dot(p, v_ext, preferred_element_type=F32)
            put(acc_scr, cur, acc[:, :HEAD_DIM])
            put(m_scr, cur, jnp.broadcast_to(m, (Q_BLOCK, HEAD_DIM)))
            put(l_scr, cur, acc[:, HEAD_DIM:])
            return carry

        lax.fori_loop(0, dil * n_blk, body, 0, unroll=ATTN_UNROLL)

    n_pat = len(WINDOWS)

    def merge(c, carry):
        sl = pl.ds(pl.multiple_of(c * MERGE_ROWS, MERGE_ROWS), MERGE_ROWS)
        ms = [m_scr[p, sl, :] for p in range(n_pat)]
        top = functools.reduce(jnp.maximum, ms)
        ws = [jnp.exp(m - top) for m in ms]
        num = functools.reduce(jnp.add, [w * acc_scr[p, sl, :] for p, w in enumerate(ws)])
        den = functools.reduce(jnp.add, [w * l_scr[p, sl, :] for p, w in enumerate(ws)])
        o_ref[0, sl, :] = _head_norm(num / den, g_ref[0])
        return carry

    lax.fori_loop(0, seq // MERGE_ROWS, merge, 0, unroll=MERGE_UNROLL)


def _attn_prompt(q3, k3, v3, head_g):
    b, seq, _ = q3.shape
    spec = pl.BlockSpec((1, seq, HEAD_DIM), lambda i, h: (i, 0, h))
    return pl.pallas_call(
        _attn_prompt_kernel,
        grid=(b, N_HEADS),
        in_specs=[spec, spec, spec, pl.BlockSpec((1, 1, HEAD_DIM), lambda i, h: (h, 0, 0))],
        out_specs=spec,
        out_shape=jax.ShapeDtypeStruct((b, seq, ATT_W), F32),
        scratch_shapes=[pltpu.VMEM((len(WINDOWS), seq, HEAD_DIM), F32)] * 3,
        compiler_params=_params(("arbitrary", "arbitrary"), 32),
        name="attn_prompt",
    )(q3, k3, v3, head_g.reshape(N_HEADS, 1, HEAD_DIM))


S_PAD = 8
TAIL = 512
N_COLS = LANES


SAMPLE_PER_STEP = 2


def _attn_sample_kernel(q_ref, kn_ref, vn_ref, kt_ref, vt_ref, kd_ref, vd_ref, g_ref, o_ref, *scr,
                        n_new, past):
    for e in range(SAMPLE_PER_STEP):
        _attn_sample_one(q_ref.at[e:e + 1], kn_ref.at[e:e + 1], vn_ref.at[e:e + 1], kt_ref.at[e:e + 1],
                         vt_ref.at[e:e + 1], kd_ref.at[e:e + 1], vd_ref.at[e:e + 1], g_ref,
                         o_ref.at[e:e + 1], scr[2 * e], scr[2 * e + 1], n_new=n_new, past=past)


def _attn_sample_one(q_ref, kn_ref, vn_ref, kt_ref, vt_ref, kd_ref, vd_ref, g_ref, o_ref,
                     k_scr, v_scr, *, n_new, past):
    n_dil = past // DILATIONS[2]
    kd = jnp.swapaxes(kd_ref[0], 0, 1)
    vd = jnp.swapaxes(vd_ref[0], 0, 1)
    kt = jnp.swapaxes(kt_ref[0].reshape(TAIL, N_HEADS, HEAD_DIM), 0, 1)
    vt = jnp.swapaxes(vt_ref[0].reshape(TAIL, N_HEADS, HEAD_DIM), 0, 1)
    for h in range(N_HEADS):
        lanes = slice(h * HEAD_DIM, (h + 1) * HEAD_DIM)
        k_scr[0:TAIL, lanes] = kt[h].astype(BF16)
        v_scr[0:TAIL, lanes] = vt[h].astype(BF16)
        for res in range(n_new):
            dst = slice(TAIL + res * n_dil, TAIL + (res + 1) * n_dil)
            k_scr[dst, lanes] = kd[res * N_HEADS + h].astype(BF16)
            v_scr[dst, lanes] = vd[res * N_HEADS + h].astype(BF16)
    dil0 = TAIL

    q8 = q_ref[0]
    qt = jnp.concatenate([q8] * N_HEADS + [jnp.zeros_like(q8)] * (N_COLS // S_PAD - N_HEADS), axis=0)
    row = lax.broadcasted_iota(jnp.int32, (N_COLS, ATT_W), 0)
    lane = lax.broadcasted_iota(jnp.int32, (N_COLS, ATT_W), 1)
    qbd = jnp.where(row // S_PAD == lane // HEAD_DIM, qt, 0.0).astype(BF16)

    def col_query(shape):
        return lax.broadcasted_iota(jnp.int32, shape, 1) % S_PAD % n_new

    def key_row(shape):
        return lax.broadcasted_iota(jnp.int32, shape, 0)

    groups = []
    s = _nt_dot(k_scr[0:TAIL, :], qbd) * ATT_SCALE
    r, c = key_row(s.shape), col_query(s.shape)
    base = past - TAIL
    mult = ((((base + r) % DILATIONS[1] == (past + c) % DILATIONS[1])
             & (base + r >= past + c - WINDOWS[1])).astype(F32)
            + (base + r >= past + c - WINDOWS[0]).astype(F32))
    groups.append((s, mult, v_scr[0:TAIL, :]))
    for res in range(n_new):
        rows = slice(dil0 + res * n_dil, dil0 + (res + 1) * n_dil)
        s = _nt_dot(k_scr[rows, :], qbd) * ATT_SCALE
        mult = (col_query(s.shape) == res).astype(F32)
        groups.append((s, mult, v_scr[rows, :]))
    s = _nt_dot(kn_ref[0].astype(BF16), qbd) * ATT_SCALE
    r, c = key_row(s.shape), col_query(s.shape)
    mult = (r <= c).astype(F32) + 2.0 * (r == c).astype(F32)
    groups.append((s, mult, vn_ref[0].astype(BF16)))

    m = None
    for s, mult, _ in groups:
        gm = jnp.max(jnp.where(mult > 0, s, NEG_INF), axis=0, keepdims=True)
        m = gm if m is None else jnp.maximum(m, gm)
    num = jnp.zeros((N_COLS, ATT_W), F32)
    den = jnp.zeros((N_COLS, LANES), F32)
    for s, mult, v in groups:
        p = mult * jnp.exp(jnp.where(mult > 0, s - m, NEG_INF))
        pt = p.T.astype(BF16)
        num = num + jnp.dot(pt, v, preferred_element_type=F32)
        den = den + jnp.dot(pt, jnp.ones((v.shape[0], LANES), BF16), preferred_element_type=F32)
    outs = []
    for h in range(N_HEADS):
        o = (num[h * S_PAD:(h + 1) * S_PAD, h * HEAD_DIM:(h + 1) * HEAD_DIM]
             / den[h * S_PAD:(h + 1) * S_PAD, :])
        outs.append(_head_norm(o, g_ref[h]))
    o_ref[0] = jnp.concatenate(outs, axis=1)


def _attn_sample(q8, k8, v8, cache_k, cache_v, head_g, n_new, past):
    b = q8.shape[0]
    lw = cache_k.shape[1]
    assert lw == past and lw % 16 == 0 and lw >= WINDOWS[2] and n_new <= 4
    dil = DILATIONS[2]
    ck_rows = cache_k.reshape(b, lw * N_HEADS, HEAD_DIM)
    cv_rows = cache_v.reshape(b, lw * N_HEADS, HEAD_DIM)
    ck_grp = cache_k.reshape(b, lw // dil, dil * N_HEADS, HEAD_DIM)
    cv_grp = cache_v.reshape(b, lw // dil, dil * N_HEADS, HEAD_DIM)
    nb = SAMPLE_PER_STEP
    assert b % nb == 0
    new_spec = pl.BlockSpec((nb, S_PAD, ATT_W), lambda i: (i, 0, 0))
    tail_spec = pl.BlockSpec((nb, TAIL * N_HEADS, HEAD_DIM), lambda i: (i, lw // TAIL - 1, 0))
    dil_spec = pl.BlockSpec((nb, lw // dil, n_new * N_HEADS, HEAD_DIM), lambda i: (i, 0, 0, 0))
    n_keys = TAIL + n_new * (lw // dil)
    return pl.pallas_call(
        functools.partial(_attn_sample_kernel, n_new=n_new, past=past),
        grid=(b // nb,),
        in_specs=[new_spec, new_spec, new_spec, tail_spec, tail_spec, dil_spec, dil_spec,
                  pl.BlockSpec((N_HEADS, 1, HEAD_DIM), lambda i: (0, 0, 0))],
        out_specs=new_spec,
        out_shape=jax.ShapeDtypeStruct((b, S_PAD, ATT_W), F32),
        scratch_shapes=[pltpu.VMEM((n_keys, ATT_W), BF16)] * (2 * nb),
        compiler_params=_params(("arbitrary",), 48),
        name="attn_sample",
    )(q8, k8, v8, ck_rows, cv_rows, ck_grp, cv_grp, head_g.reshape(N_HEADS, 1, HEAD_DIM))


def _finish_kernel(x_ref, conv_ref, attn_ref, g1_ref, sh2_ref, sc2_ref, lng_ref, lnb_ref, n2g_ref,
                   wout_ref, wq_ref, x1_ref, h2t_ref, qp_ref, cat_scr):
    y = conv_ref[...]
    mu = jnp.mean(y, axis=-1, keepdims=True)
    yc = y - mu
    var = jnp.mean(yc * yc, axis=-1, keepdims=True)
    z = yc * lax.rsqrt(var + EPS) * lng_ref[...] + lnb_ref[...]
    cat_scr[:, :CONV_CH] = (z * jax.nn.sigmoid(z)).astype(BF16)
    cat_scr[:, CONV_CH:] = attn_ref[...].astype(BF16)
    mix = jnp.dot(cat_scr[...], wout_ref[...], preferred_element_type=F32)
    x1 = x_ref[...] + (1.0 + _rows(g1_ref)) * mix
    x1_ref[...] = x1
    n = x1 * lax.rsqrt(jnp.mean(x1 * x1, axis=-1, keepdims=True) + EPS)
    h2 = (n * n2g_ref[...]) * (1.0 + _rows(sc2_ref)) + _rows(sh2_ref)
    h2t_ref[...] = h2.T.astype(BF16)
    qp_ref[...] = jnp.dot(h2.astype(BF16), wq_ref[...], preferred_element_type=F32).astype(BF16)


def _finish(x, conv_raw, attn, mods, group_rows, ln_g, ln_b, n2_g, wout_bf, wq_bf, tm):
    t = x.shape[0]
    row = lambda n: pl.BlockSpec((1, n), lambda i: (0, 0))
    const = lambda shape: pl.BlockSpec(shape, lambda i: (0, 0), pipeline_mode=pl.Buffered(1))
    tile = lambda n: pl.BlockSpec((tm, n), lambda i: (i, 0))
    return pl.pallas_call(
        _finish_kernel,
        grid=(t // tm,),
        in_specs=[tile(D_MODEL), tile(CONV_CH), tile(ATT_W),
                  _mod_spec(mods, 2, tm, group_rows, 1),
                  _mod_spec(mods, 3, tm, group_rows, 1),
                  _mod_spec(mods, 4, tm, group_rows, 1),
                  row(CONV_CH), row(CONV_CH), row(D_MODEL),
                  const((D_MODEL, D_MODEL)), const((D_MODEL, D_MODEL))],
        out_specs=[tile(D_MODEL), pl.BlockSpec((D_MODEL, tm), lambda i: (0, i)), tile(D_MODEL)],
        out_shape=[jax.ShapeDtypeStruct((t, D_MODEL), F32),
                   jax.ShapeDtypeStruct((D_MODEL, t), BF16),
                   jax.ShapeDtypeStruct((t, D_MODEL), BF16)],
        scratch_shapes=[pltpu.VMEM((tm, D_MODEL), BF16)],
        compiler_params=_params(("arbitrary",), 56),
        name="finish",
    )(x, conv_raw, attn, mods, mods, mods, ln_g.reshape(1, CONV_CH), ln_b.reshape(1, CONV_CH),
      n2_g.reshape(1, D_MODEL), wout_bf, wq_bf)


TOPK_RANK = PEER_TOPK + 1
TOPK_ROWS = -(-TOPK_RANK // SUBLANES) * SUBLANES


def _all_sublanes(x, combine):
    shift = SUBLANES // 2
    while shift:
        x = combine(x, pltpu.roll(x, shift, 0))
        shift //= 2
    return x


def _sorting_network(n):
    pairs = []

    def merge(lo, length, r):
        step = 2 * r
        if step < length:
            merge(lo, length, step)
            merge(lo + r, length, step)
            pairs.extend((i, i + r) for i in range(lo + r, lo + length - r, step))
        else:
            pairs.append((lo, lo + r))

    def sort(lo, length):
        if length > 1:
            sort(lo, length // 2)
            sort(lo + length // 2, length // 2)
            merge(lo, length, 1)

    sort(0, n)
    return pairs


def _top_values(s, k, out_scr):
    n_tiles = s.shape[0] // SUBLANES
    v = [s[r * SUBLANES:(r + 1) * SUBLANES, :] for r in range(n_tiles)]
    for i, j in _sorting_network(n_tiles):
        v[i], v[j] = jnp.maximum(v[i], v[j]), jnp.minimum(v[i], v[j])
    out_scr[...] = jnp.full(out_scr.shape, NEG_INF, F32)
    depth = min(n_tiles, k)
    v = v[:depth]
    for i in range(k):
        m = _all_sublanes(v[0], jnp.maximum)
        out_scr[i:i + 1, :] = m[0:1, :]
        pop = v[0] == m
        live = min(depth, k - i)
        for r in range(live):
            below = v[r + 1] if r + 1 < depth else NEG_INF
            v[r] = jnp.where(pop, below, v[r])


def _pair_candidates(a_scr, b_scr):
    tb = a_scr.shape[1]
    rows_all = lax.broadcasted_iota(jnp.int32, (TOPK_ROWS, tb), 0)
    rows_one = lax.broadcasted_iota(jnp.int32, (SUBLANES, tb), 0)
    cands = [a_scr[0:1, :] + b_scr[...],
             jnp.where(rows_all >= 1, b_scr[0:1, :] + a_scr[...], NEG_INF)]
    single = []
    for i in range(1, TOPK_RANK):
        j_max = TOPK_RANK // (i + 1) - 1
        if j_max >= 2:
            assert j_max < SUBLANES
            ok = (rows_one >= 1) & (rows_one <= j_max)
            cands.append(jnp.where(ok, a_scr[i:i + 1, :] + b_scr[0:SUBLANES, :], NEG_INF))
        elif j_max == 1:
            single.append(i)
    if single:
        assert single == list(range(single[0], single[-1] + 1)) and single[-1] < SUBLANES
        ok = (rows_one >= single[0]) & (rows_one <= single[-1])
        cands.append(jnp.where(ok, b_scr[1:2, :] + a_scr[0:SUBLANES, :], NEG_INF))
    return [c[r:r + SUBLANES, :] for c in cands for r in range(0, c.shape[0], SUBLANES)]


def _column_reduce(tiles, combine):
    return _all_sublanes(functools.reduce(combine, tiles), combine)


def _topk_kernel(qp_ref, keys_ref, s1_ref, s2_ref, e1_ref, e2_ref, thr_ref, a_scr, b_scr):
    tb = qp_ref.shape[0]
    for h in range(PEER_HEADS):
        sc = []
        for p in range(2):
            col = (h * 2 + p) * PEER_DK
            sc.append(_nt_dot(keys_ref[h, p], qp_ref[:, col:col + PEER_DK]))
        s1, s2 = sc
        _top_values(s1, TOPK_RANK, a_scr)
        _top_values(s2, TOPK_RANK, b_scr)
        cands = _pair_candidates(a_scr, b_scr)
        top = None
        cum = jnp.zeros((SUBLANES, tb), F32)
        z = jnp.zeros((SUBLANES, tb), F32)
        v_in = jnp.zeros((SUBLANES, tb), F32)
        v_out = jnp.zeros((SUBLANES, tb), F32)
        for _ in range(TOPK_RANK):
            m = _column_reduce(cands, jnp.maximum)
            hits = [c == m for c in cands]
            cnt = _column_reduce([jnp.where(e, 1.0, 0.0) for e in hits], jnp.add)
            cands = [jnp.where(e, NEG_INF, c) for e, c in zip(hits, cands)]
            top = m if top is None else top
            take = jnp.minimum(cnt, jnp.maximum(PEER_TOPK - cum, 0.0))
            z = z + take * jnp.exp(m - top)
            reached = cum + cnt
            v_in = jnp.where((cum < PEER_TOPK) & (reached >= PEER_TOPK), m, v_in)
            v_out = jnp.where((cum < TOPK_RANK) & (reached >= TOPK_RANK), m, v_out)
            cum = reached
        s1_ref[h] = s1
        s2_ref[h] = s2
        e1_ref[h] = jnp.exp(s1 - a_scr[0:1, :])
        e2_ref[h] = jnp.exp(s2 - b_scr[0:1, :]) / z[0:1, :]
        thr_ref[h] = 0.5 * (v_in[0:1, :] + v_out[0:1, :])


def _topk(qp, keys_bf, tb):
    t = qp.shape[0]
    big = pl.BlockSpec((PEER_HEADS, PEER_NKEYS, tb), lambda i: (0, 0, i))
    big_sds = jax.ShapeDtypeStruct((PEER_HEADS, PEER_NKEYS, t), F32)
    return pl.pallas_call(
        _topk_kernel,
        grid=(t // tb,),
        in_specs=[pl.BlockSpec((tb, D_MODEL), lambda i: (i, 0)),
                  pl.BlockSpec((PEER_HEADS, 2, PEER_NKEYS, PEER_DK), lambda i: (0, 0, 0, 0))],
        out_specs=[big, big, big, big, pl.BlockSpec((PEER_HEADS, 1, tb), lambda i: (0, 0, i))],
        out_shape=[big_sds, big_sds, big_sds, big_sds, jax.ShapeDtypeStruct((PEER_HEADS, 1, t), F32)],
        scratch_shapes=[pltpu.VMEM((TOPK_ROWS, tb), F32), pltpu.VMEM((TOPK_ROWS, tb), F32)],
        compiler_params=_params(("arbitrary",), 32),
        name="peer_topk",
    )(qp, keys_bf)


_SQRT_HALF = float(np.sqrt(0.5))


def _gelu(x):
    return 0.5 * x * (1.0 + lax.erf(x * _SQRT_HALF))


MXU_TILE = 256
HID_GROUPS = (512, 512)


def _peer_kernel(ht_ref, u_ref, vt_ref, s1_ref, s2_ref, e1_ref, e2_ref, thr_ref, o_ref, acc_scr, *w_refs):
    j = pl.program_id(1)
    tb = ht_ref.shape[1]
    ec = u_ref.shape[0]
    assert ec == SUBLANES * PEER_NKEYS
    n_k, n_n = ec // MXU_TILE, tb // MXU_TILE
    assert len(w_refs) == n_k * n_n

    @pl.when(j == 0)
    def _():
        acc_scr[...] = jnp.zeros_like(acc_scr)

    row0 = pl.multiple_of(j * SUBLANES, SUBLANES)
    assert sum(HID_GROUPS) == ec
    starts = [sum(HID_GROUPS[:g]) for g in range(len(HID_GROUPS))]
    hids = [jnp.dot(u_ref[s:s + n, :], ht_ref[...], preferred_element_type=F32)
            for s, n in zip(starts, HID_GROUPS)]
    per_tile = MXU_TILE // PEER_NKEYS
    for k in range(n_k):
        for n in range(n_n):
            w_ref = w_refs[k * n_n + n]
            for a in range(per_tile):
                ii = k * per_tile + a
                q = max(g for g, s in enumerate(starts) if s <= ii * PEER_NKEYS)
                r = ii * PEER_NKEYS - starts[q]
                for b in range(MXU_TILE // LANES):
                    lc = n * (MXU_TILE // LANES) + b
                    sl = slice(lc * LANES, (lc + 1) * LANES)
                    gate = jnp.zeros((PEER_NKEYS, LANES), F32)
                    for h in range(PEER_HEADS):
                        s1_rows = s1_ref[h, pl.ds(row0, SUBLANES), sl]
                        e1_rows = e1_ref[h, pl.ds(row0, SUBLANES), sl]
                        bound = thr_ref[h, :, sl] - s1_rows[ii:ii + 1, :]
                        val = e2_ref[h, :, sl] * e1_rows[ii:ii + 1, :]
                        gate = gate + jnp.where(s2_ref[h, :, sl] >= bound, val, 0.0)
                    hid = hids[q][r:r + PEER_NKEYS, sl]
                    w_ref[a * PEER_NKEYS:(a + 1) * PEER_NKEYS, b * LANES:(b + 1) * LANES] = (
                        gate * _gelu(hid)).astype(BF16)
    w_all = jnp.concatenate([jnp.concatenate([w_refs[k * n_n + n][...] for n in range(n_n)], axis=1)
                             for k in range(n_k)], axis=0)
    acc_scr[...] += jnp.dot(vt_ref[...], w_all, preferred_element_type=F32)

    @pl.when(j == pl.num_programs(1) - 1)
    def _():
        o_ref[...] = acc_scr[...].T


def _peer(h2t, u_bf, vt_bf, s1, s2, e1, e2, thr, tb, ec):
    t = h2t.shape[1]
    n_chunks = u_bf.shape[0] // ec
    big = pl.BlockSpec((PEER_HEADS, PEER_NKEYS, tb), lambda i, j: (0, 0, i))
    n_tiles = (ec // MXU_TILE) * (tb // MXU_TILE)
    return pl.pallas_call(
        _peer_kernel,
        grid=(t // tb, n_chunks),
        in_specs=[pl.BlockSpec((D_MODEL, tb), lambda i, j: (0, i)),
                  pl.BlockSpec((ec, D_MODEL), lambda i, j: (j, 0)),
                  pl.BlockSpec((D_MODEL, ec), lambda i, j: (0, j)),
                  big, big, big, big,
                  pl.BlockSpec((PEER_HEADS, 1, tb), lambda i, j: (0, 0, i))],
        out_specs=pl.BlockSpec((tb, D_MODEL), lambda i, j: (i, 0)),
        out_shape=jax.ShapeDtypeStruct((t, D_MODEL), F32),
        scratch_shapes=[pltpu.VMEM((D_MODEL, tb), F32)] + [pltpu.VMEM((MXU_TILE, MXU_TILE), BF16)] * n_tiles,
        compiler_params=_params(("arbitrary", "arbitrary"), 56),
        name="peer",
    )(h2t, u_bf, vt_bf, s1, s2, e1, e2, thr)


def _final_kernel(x1_ref, p_ref, g2_ref, fg_ref, y_ref):
    x2 = x1_ref[...] + (1.0 + _rows(g2_ref)) * p_ref[...]
    y_ref[...] = x2 * lax.rsqrt(jnp.mean(x2 * x2, axis=-1, keepdims=True) + EPS) * fg_ref[...]


def _final(x1, peer_out, mods, group_rows, final_g, tm):
    t = x1.shape[0]
    tile = pl.BlockSpec((tm, D_MODEL), lambda i: (i, 0))
    return pl.pallas_call(
        _final_kernel,
        grid=(t // tm,),
        in_specs=[tile, tile, _mod_spec(mods, 5, tm, group_rows, 1),
                  pl.BlockSpec((1, D_MODEL), lambda i: (0, 0))],
        out_specs=tile,
        out_shape=jax.ShapeDtypeStruct((t, D_MODEL), F32),
        compiler_params=_params(("arbitrary",), 48),
        name="final",
    )(x1, peer_out, mods, final_g.reshape(1, D_MODEL))


PAST_LEN = 2048
MIXIN_TOKENS = 256
TOKEN_TILE = 256
PEER_TOKENS = 512
PEER_CHUNK = 1024
TOPK_TOKENS = 256


def _channel_tail(x, conv_raw, attn, mods, group_rows, weights):
    x1, h2t, qp = _finish(x, conv_raw, attn, mods, group_rows, weights["ln_g"], weights["ln_b"],
                          weights["n2_g"], weights["wout"], weights["wq"], TOKEN_TILE)
    s1, s2, e1, e2, thr = _topk(qp, weights["keys"], TOPK_TOKENS)
    peer_out = _peer(h2t, weights["u"], weights["vt"], s1, s2, e1, e2, thr, PEER_TOKENS, PEER_CHUNK)
    return _final(x1, peer_out, mods, group_rows, weights["final_g"], TOKEN_TILE)


def kernel(x_prompt, x_sample, cache_k, cache_v, state_conv, c_prompt, c_sample, w_ada, b_ada, norm1_g, w_in,
           conv_w, conv_b, conv_ln_g, conv_ln_b, attn_out_g, w_out, norm2_g, peer_wq, peer_keys, peer_u, peer_v,
           final_g):
    b, seq, d = x_prompt.shape
    db, n_new, _ = x_sample.shape
    depth = w_ada.shape[0]
    assert depth == 1 and d == D_MODEL
    cache_len = cache_k.shape[2]
    n_sample = db * n_new
    assert seq % MIXIN_TOKENS == 0 and n_sample % PEER_TOKENS == 0

    c_rows = jnp.concatenate([c_prompt, jnp.repeat(c_sample, n_new, axis=0)], axis=0)
    mods = _ada(c_rows, w_ada[0], b_ada[0])
    mods_p = mods[:b].reshape(b, 1, N_ADA * D_MODEL)
    mods_s = mods[b:]

    weights = {
        "ln_g": conv_ln_g[0], "ln_b": conv_ln_b[0], "n2_g": norm2_g[0],
        "wout": w_out[0].astype(BF16), "wq": peer_wq[0].astype(BF16),
        "keys": peer_keys[0].astype(BF16), "u": peer_u[0].astype(BF16),
        "vt": peer_v[0].T.astype(BF16), "final_g": final_g,
    }
    w_in_bf = w_in[0].astype(BF16)

    xp = x_prompt.reshape(b * seq, d)
    tables_p = _rope_tables(np.arange(seq))
    glu, q, k, v = _mixin(xp, mods_p, seq, norm1_g[0], w_in_bf, tables_p, MIXIN_TOKENS)
    glu3 = glu.reshape(b, seq, CONV_CH)
    conv_raw = _conv_prompt(glu3, jnp.zeros((b, CONV_PAD, CONV_CH), F32), conv_w[0], conv_b[0])
    attn = _attn_prompt(q.reshape(b, seq, ATT_W), k.reshape(b, seq, ATT_W), v.reshape(b, seq, ATT_W),
                        attn_out_g[0])
    y_prompt = _channel_tail(xp, conv_raw.reshape(b * seq, CONV_CH), attn.reshape(b * seq, ATT_W),
                             mods_p, seq, weights).reshape(b, seq, d)
    keep = min(WINDOWS[2], seq)
    new_k_prompt = k.reshape(b, seq, N_HEADS, HEAD_DIM)[:, seq - keep:][None]
    new_v_prompt = v.reshape(b, seq, N_HEADS, HEAD_DIM)[:, seq - keep:][None]
    new_conv_prompt = glu3[:, seq - CONV_HIST:][None]

    xs = x_sample.reshape(n_sample, d)
    tables_s = tuple(jnp.tile(t, (TOKEN_TILE // n_new, 1)) for t in _rope_tables(PAST_LEN + np.arange(n_new)))
    glu_s, q_s, k_s, v_s = _mixin(xs, mods_s, None, norm1_g[0], w_in_bf, tables_s, TOKEN_TILE)
    glu_s3 = glu_s.reshape(db, n_new, CONV_CH)
    conv_raw_s = _conv_sample(state_conv[0], glu_s3, conv_w[0], conv_b[0])
    pad8 = lambda a: jnp.pad(a.reshape(db, n_new, ATT_W), ((0, 0), (0, S_PAD - n_new), (0, 0)))
    attn_s = _attn_sample(pad8(q_s), pad8(k_s), pad8(v_s), cache_k[0], cache_v[0], attn_out_g[0], n_new,
                          cache_len)
    attn_s = attn_s[:, :n_new].reshape(n_sample, ATT_W)
    y_sample = _channel_tail(xs, conv_raw_s.reshape(n_sample, CONV_CH), attn_s,
                             mods_s, None, weights).reshape(db, n_new, d)
    new_k_sample = k_s.reshape(db, n_new, N_HEADS, HEAD_DIM)[None]
    new_v_sample = v_s.reshape(db, n_new, N_HEADS, HEAD_DIM)[None]
    new_conv_sample = jnp.concatenate([state_conv[0], glu_s3], axis=1)[:, n_new:][None]

    return (y_prompt, y_sample, new_k_prompt, new_v_prompt, new_conv_prompt,
            new_k_sample, new_v_sample, new_conv_sample)
```
